```python
import jax, jax.numpy as jnp
from jax import lax
import numpy as np

D_MODEL = 2048
BATCH = 4
SEQ = 2048
DEPTH = 2
DEC_BATCH = 128
DEC_SEQ = 1
PAST_LEN = 2048
PAGE_SIZE = 128

HEAD_DIM = 128
N_HEADS = D_MODEL // HEAD_DIM
H_SB = N_HEADS // 2
H_MOBA = N_HEADS - H_SB
H_FOX = N_HEADS
D_FF = 4 * D_MODEL
ROT_DIM = HEAD_DIM // 4
ROPE_THETA = 500000.0
MOBA_BLOCK = 256
MOBA_TOPK = 3
Q_BLOCK = 128
MOBA_Q_BLOCK = 32
EPS = 1e-6
FORGET_BIAS_INIT = 3.0
POOL_EXTRA_FRAC = 4

kernel_name = "hybrid_sb_moba_fox_decoder_step"


def rmsnorm(x, g):
    xf = x.astype(jnp.float32)
    y = xf * lax.rsqrt(jnp.mean(xf * xf, axis=-1, keepdims=True) + EPS)
    return (y * g.astype(jnp.float32)).astype(x.dtype)


def rope_partial(x, pos):
    inv = ROPE_THETA ** (-jnp.arange(0, ROT_DIM, 2, dtype=jnp.float32) / ROT_DIM)
    ang = pos.astype(jnp.float32)[:, None] * inv[None, :]
    cos = jnp.cos(ang)[None, :, None, :]
    sin = jnp.sin(ang)[None, :, None, :]
    xr = x[..., :ROT_DIM].astype(jnp.float32)
    x1, x2 = xr[..., :ROT_DIM // 2], xr[..., ROT_DIM // 2:]
    rot = jnp.concatenate([x1 * cos - x2 * sin, x2 * cos + x1 * sin], axis=-1).astype(x.dtype)
    return jnp.concatenate([rot, x[..., ROT_DIM:]], axis=-1)


def sqrelu_mlp(x, w_up, w_down):
    h = jax.nn.relu(x @ w_up)
    return (h * h) @ w_down


def split_heads(proj, head_counts):
    cuts = np.cumsum([n * HEAD_DIM for n in head_counts])[:-1].tolist()
    parts = jnp.split(proj, cuts, axis=-1)
    return [p.reshape(*p.shape[:-1], n, HEAD_DIM) for p, n in zip(parts, head_counts)]


def merge_out(outs, w_out):
    o = jnp.concatenate([o.reshape(*o.shape[:2], -1) for o in outs], axis=-1)
    return o @ w_out


def merge_blocks(o):
    nb, b, tb, h, d = o.shape
    return o.transpose(1, 0, 2, 3, 4).reshape(b, nb * tb, h, d)


def gather_pages(pool, page_table):
    g = pool[page_table]
    return g.reshape(g.shape[0], -1, *pool.shape[2:])


def scores(q, k):
    return jnp.einsum('bthd,bshd->bhts', q, k, preferred_element_type=jnp.float32) * (HEAD_DIM ** -0.5)


def stick_breaking_weights(z, mask):
    ax = z.ndim - 1
    log_beta = jax.nn.log_sigmoid(z)
    log_keep = jnp.where(mask, jax.nn.log_sigmoid(-z), 0.0)
    rest = lax.cumsum(log_keep, axis=ax, reverse=True) - log_keep
    return jnp.where(mask, jnp.exp(log_beta + rest), 0.0)


def sb_prompt(q, k, v):
    S = q.shape[1]
    k_pos = jnp.arange(S)

    def block(i):
        t0 = i * Q_BLOCK
        qb = lax.dynamic_slice_in_dim(q, t0, Q_BLOCK, axis=1)
        q_pos = t0 + jnp.arange(Q_BLOCK)
        a = stick_breaking_weights(scores(qb, k), k_pos[None, :] < q_pos[:, None])
        return jnp.einsum('bhts,bshd->bthd', a.astype(v.dtype), v)

    return merge_blocks(lax.map(block, jnp.arange(S // Q_BLOCK)))


def sb_sample(q, k_new, v_new, k_past, v_past):
    T, P = q.shape[1], k_past.shape[1]
    z = jnp.concatenate([scores(q, k_past), scores(q, k_new)], axis=-1)
    t = jnp.arange(T)
    mask = jnp.concatenate([jnp.ones((T, P), bool), t[None, :] < t[:, None]], axis=-1)
    a = stick_breaking_weights(z, mask).astype(v_new.dtype)
    return (jnp.einsum('bhts,bshd->bthd', a[..., :P], v_past)
            + jnp.einsum('bhts,bshd->bthd', a[..., P:], v_new))


def block_means(k, n_blocks):
    b, _, h, d = k.shape
    m = k[:, :n_blocks * MOBA_BLOCK].astype(jnp.float32).reshape(b, n_blocks, MOBA_BLOCK, h, d).mean(axis=2)
    m = jnp.pad(m, ((0, 0), (0, max(n_blocks, MOBA_TOPK) - n_blocks), (0, 0), (0, 0)))
    return m.astype(k.dtype)


def moba_core(q, q_pos, k_means, fetch):
    nbm = k_means.shape[1]
    own = q_pos // MOBA_BLOCK
    gate = jnp.einsum('bthd,bnhd->bhtn', q, k_means, preferred_element_type=jnp.float32)
    past_ok = jnp.arange(nbm)[None, :] < own[:, None]
    gate = jnp.where(past_ok, gate, -jnp.inf)
    gval, gidx = lax.top_k(gate, MOBA_TOPK)
    sel_ok = gval > -jnp.inf
    own_b = jnp.broadcast_to(own[None, None, :, None], gidx.shape[:3] + (1,)).astype(gidx.dtype)
    blocks = jnp.concatenate([gidx, own_b], axis=-1)
    pos = (blocks[..., None] * MOBA_BLOCK + jnp.arange(MOBA_BLOCK)).reshape(*blocks.shape[:3], -1)
    ok = jnp.concatenate([jnp.repeat(sel_ok, MOBA_BLOCK, axis=-1),
                          pos[..., MOBA_TOPK * MOBA_BLOCK:] <= q_pos[None, None, :, None]], axis=-1)
    kg, vg = fetch(pos)
    z = jnp.einsum('bthd,bhtnd->bhtn', q, kg, preferred_element_type=jnp.float32) * (HEAD_DIM ** -0.5)
    p = jax.nn.softmax(jnp.where(ok, z, -jnp.inf), axis=-1)
    return jnp.einsum('bhtn,bhtnd->bthd', p.astype(vg.dtype), vg)


def moba_prompt(q, k, v):
    S = q.shape[1]
    nb = -(-S // MOBA_BLOCK)
    L = nb * MOBA_BLOCK
    pad = ((0, 0), (0, L - S), (0, 0), (0, 0))
    k_pad, v_pad = jnp.pad(k, pad), jnp.pad(v, pad)
    k_means = block_means(k_pad, nb)

    def fetch(pos):
        pos = jnp.minimum(pos, L - 1)
        def per_seq(kk, vv, pp):
            g = lambda a: jax.vmap(lambda a_h, p_h: a_h[p_h], in_axes=(1, 0))(a, pp)
            return g(kk), g(vv)
        return jax.vmap(per_seq)(k_pad, v_pad, pos)

    def block(i):
        t0 = i * MOBA_Q_BLOCK
        qb = lax.dynamic_slice_in_dim(q, t0, MOBA_Q_BLOCK, axis=1)
        return moba_core(qb, t0 + jnp.arange(MOBA_Q_BLOCK), k_means, fetch)

    return merge_blocks(lax.map(block, jnp.arange(S // MOBA_Q_BLOCK)))


def moba_sample(q, k_new, v_new, k_pool, v_pool, page_table, q_pos):
    P = page_table.shape[1] * PAGE_SIZE
    T, H = q.shape[1], q.shape[2]
    k_means = block_means(gather_pages(k_pool, page_table), P // MOBA_BLOCK)

    def fetch(pos):
        def per_seq(pt, kn, vn, pp):
            def per_head(h, ph):
                row = jnp.minimum(ph, P - 1)
                phys, off = pt[row // PAGE_SIZE], row % PAGE_SIZE
                is_new = (ph >= P)[..., None]
                j = jnp.clip(ph - P, 0, T - 1)
                kk = jnp.where(is_new, kn[j, h], k_pool[phys, off, h])
                vv = jnp.where(is_new, vn[j, h], v_pool[phys, off, h])
                return kk, vv
            return jax.vmap(per_head)(jnp.arange(H), pp)
        return jax.vmap(per_seq)(page_table, k_new, v_new, pos)

    return moba_core(q, q_pos, k_means, fetch)


def fox_weights(z, bias, mask):
    return jax.nn.softmax(jnp.where(mask, z + bias, -jnp.inf), axis=-1)


def fox_prompt(q, k, v, logf):
    S = q.shape[1]
    F = jnp.cumsum(logf.astype(jnp.float32), axis=1).transpose(0, 2, 1)
    k_pos = jnp.arange(S)

    def block(i):
        t0 = i * Q_BLOCK
        qb = lax.dynamic_slice_in_dim(q, t0, Q_BLOCK, axis=1)
        Ft = lax.dynamic_slice_in_dim(F, t0, Q_BLOCK, axis=2)
        q_pos = t0 + jnp.arange(Q_BLOCK)
        bias = Ft[..., :, None] - F[..., None, :]
        p = fox_weights(scores(qb, k), bias, k_pos[None, :] <= q_pos[:, None])
        return jnp.einsum('bhts,bshd->bthd', p.astype(v.dtype), v)

    return merge_blocks(lax.map(block, jnp.arange(S // Q_BLOCK)))


def fox_sample(q, k_new, v_new, logf_new, k_past, v_past, logf_past):
    T, P = q.shape[1], k_past.shape[1]
    F_past = jnp.cumsum(logf_past.astype(jnp.float32), axis=1)
    F_new = F_past[:, -1:] + jnp.cumsum(logf_new.astype(jnp.float32), axis=1)
    F_all = jnp.concatenate([F_past, F_new], axis=1).transpose(0, 2, 1)
    bias = F_new.transpose(0, 2, 1)[..., :, None] - F_all[..., None, :]
    z = jnp.concatenate([scores(q, k_past), scores(q, k_new)], axis=-1)
    t = jnp.arange(T)
    mask = jnp.concatenate([jnp.ones((T, P), bool), t[None, :] <= t[:, None]], axis=-1)
    p = fox_weights(z, bias, mask).astype(v_new.dtype)
    return (jnp.einsum('bhts,bshd->bthd', p[..., :P], v_past)
            + jnp.einsum('bhts,bshd->bthd', p[..., P:], v_new))


def ab_project(hn, w_in, pos):
    qa, ka, va, qb, kb, vb = split_heads(hn @ w_in, (H_SB, H_SB, H_SB, H_MOBA, H_MOBA, H_MOBA))
    return qa, ka, va, rope_partial(qb, pos), rope_partial(kb, pos), vb


def fox_project(hn, w_in, b_f):
    n = H_FOX * HEAD_DIM
    proj = hn @ w_in
    q, k, v = split_heads(proj[..., :3 * n], (H_FOX, H_FOX, H_FOX))
    logf = jax.nn.log_sigmoid((proj[..., 3 * n:] + b_f).astype(jnp.float32))
    return q, k, v, logf


def setup_inputs(seed: int = 0) -> dict:
    key = jax.random.key(seed)
    ks = jax.random.split(key, 32)
    n_pages = PAST_LEN // PAGE_SIZE
    n_used = DEC_BATCH * n_pages
    n_phys = n_used + max(1, n_used // POOL_EXTRA_FRAC)
    nrm = lambda k, shape, scale=1.0: jax.random.normal(k, shape, jnp.float32) * scale
    page_table = jax.random.permutation(ks[0], n_phys)[:n_used].reshape(DEC_BATCH, n_pages).astype(jnp.int32)
    n_in0 = 3 * (H_SB + H_MOBA) * HEAD_DIM
    n_in1 = 3 * H_FOX * HEAD_DIM + H_FOX
    gain = lambda k: 1.0 + nrm(k, (D_MODEL,), 0.02)
    return {
        "x_prompt": nrm(ks[1], (BATCH, SEQ, D_MODEL)),
        "x_sample": nrm(ks[2], (DEC_BATCH, DEC_SEQ, D_MODEL)),
        "cache_k_sb": nrm(ks[3], (n_phys, PAGE_SIZE, H_SB, HEAD_DIM)),
        "cache_v_sb": nrm(ks[4], (n_phys, PAGE_SIZE, H_SB, HEAD_DIM)),
        "cache_k_moba": nrm(ks[5], (n_phys, PAGE_SIZE, H_MOBA, HEAD_DIM)),
        "cache_v_moba": nrm(ks[6], (n_phys, PAGE_SIZE, H_MOBA, HEAD_DIM)),
        "cache_k_fox": nrm(ks[7], (n_phys, PAGE_SIZE, H_FOX, HEAD_DIM)),
        "cache_v_fox": nrm(ks[8], (n_phys, PAGE_SIZE, H_FOX, HEAD_DIM)),
        "cache_logf_fox": jax.nn.log_sigmoid(FORGET_BIAS_INIT + nrm(ks[9], (n_phys, PAGE_SIZE, H_FOX), 0.5)),
        "page_table": page_table,
        "g_mix0": gain(ks[10]),
        "w_in0": nrm(ks[11], (D_MODEL, n_in0), D_MODEL ** -0.5),
        "w_out0": nrm(ks[12], ((H_SB + H_MOBA) * HEAD_DIM, D_MODEL), ((H_SB + H_MOBA) * HEAD_DIM) ** -0.5),
        "g_mlp0": gain(ks[13]),
        "w_up0": nrm(ks[14], (D_MODEL, D_FF), D_MODEL ** -0.5),
        "w_down0": nrm(ks[15], (D_FF, D_MODEL), D_FF ** -0.5),
        "g_mix1": gain(ks[16]),
        "w_in1": nrm(ks[17], (D_MODEL, n_in1), D_MODEL ** -0.5),
        "b_f1": FORGET_BIAS_INIT + nrm(ks[18], (H_FOX,), 0.1),
        "w_out1": nrm(ks[19], (H_FOX * HEAD_DIM, D_MODEL), (H_FOX * HEAD_DIM) ** -0.5),
        "g_mlp1": gain(ks[20]),
        "w_up1": nrm(ks[21], (D_MODEL, D_FF), D_MODEL ** -0.5),
        "w_down1": nrm(ks[22], (D_FF, D_MODEL), D_FF ** -0.5),
        "g_final": gain(ks[23]),
    }


def reference(x_prompt, x_sample, cache_k_sb, cache_v_sb, cache_k_moba, cache_v_moba,
              cache_k_fox, cache_v_fox, cache_logf_fox, page_table,
              g_mix0, w_in0, w_out0, g_mlp0, w_up0, w_down0,
              g_mix1, w_in1, b_f1, w_out1, g_mlp1, w_up1, w_down1, g_final):
    P = page_table.shape[1] * PAGE_SIZE
    pos_p = jnp.arange(x_prompt.shape[1], dtype=jnp.int32)
    pos_s = P + jnp.arange(x_sample.shape[1], dtype=jnp.int32)
    mix_gain = (g_mix0, g_mix1)
    mlp_gain = (g_mlp0, g_mlp1)
    w_up = (w_up0, w_up1)
    w_down = (w_down0, w_down1)
    hp, hs = x_prompt, x_sample
    for layer in range(DEPTH):
        hn_p, hn_s = rmsnorm(hp, mix_gain[layer]), rmsnorm(hs, mix_gain[layer])
        if layer % 2 == 0:
            qa_p, k_sb_p, v_sb_p, qb_p, k_mb_p, v_mb_p = ab_project(hn_p, w_in0, pos_p)
            mix_p = merge_out([sb_prompt(qa_p, k_sb_p, v_sb_p), moba_prompt(qb_p, k_mb_p, v_mb_p)], w_out0)
            qa_s, k_sb_s, v_sb_s, qb_s, k_mb_s, v_mb_s = ab_project(hn_s, w_in0, pos_s)
            o_sb = sb_sample(qa_s, k_sb_s, v_sb_s,
                             gather_pages(cache_k_sb, page_table), gather_pages(cache_v_sb, page_table))
            o_mb = moba_sample(qb_s, k_mb_s, v_mb_s, cache_k_moba, cache_v_moba, page_table, pos_s)
            mix_s = merge_out([o_sb, o_mb], w_out0)
        else:
            q_p, k_fox_p, v_fox_p, logf_p = fox_project(hn_p, w_in1, b_f1)
            mix_p = merge_out([fox_prompt(q_p, k_fox_p, v_fox_p, logf_p)], w_out1)
            q_s, k_fox_s, v_fox_s, logf_s = fox_project(hn_s, w_in1, b_f1)
            o_fox = fox_sample(q_s, k_fox_s, v_fox_s, logf_s,
                               gather_pages(cache_k_fox, page_table), gather_pages(cache_v_fox, page_table),
                               gather_pages(cache_logf_fox, page_table))
            mix_s = merge_out([o_fox], w_out1)
        hp = hp + mix_p
        hs = hs + mix_s
        hp = hp + sqrelu_mlp(rmsnorm(hp, mlp_gain[layer]), w_up[layer], w_down[layer])
        hs = hs + sqrelu_mlp(rmsnorm(hs, mlp_gain[layer]), w_up[layer], w_down[layer])
    y_prompt = rmsnorm(hp, g_final)
    y_sample = rmsnorm(hs, g_final)
    return (y_prompt, y_sample, k_sb_p, k_sb_s, v_sb_p, v_sb_s, k_mb_p, k_mb_s, v_mb_p, v_mb_s,
            k_fox_p, k_fox_s, v_fox_p, v_fox_s, logf_p, logf_s)
```

```python
import functools

import jax
import jax.numpy as jnp
from jax import lax
from jax.experimental import pallas as pl
from jax.experimental.pallas import tpu as pltpu

HEAD_DIM = 128
ROT_DIM = HEAD_DIM // 4
ROPE_THETA = 500000.0
MOBA_BLOCK = 256
MOBA_TOPK = 3
PAGE = 128
EPS = 1e-6
SCALE = HEAD_DIM ** -0.5
NEG_BIG = -1e30
VMEM_LIMIT = 56 * 1024 * 1024

F32 = jnp.float32
BF16 = jnp.bfloat16


def _rmsnorm(xf, g):
    ms = jnp.mean(xf * xf, axis=-1, keepdims=True)
    return xf * lax.rsqrt(ms + EPS) * g


def _log_sigmoid(z):
    return jnp.minimum(z, 0.0) - jnp.log1p(jnp.exp(-jnp.abs(z)))


def _split3(x):
    hi = x.astype(BF16)
    r = x - hi.astype(F32)
    mid = r.astype(BF16)
    lo = (r - mid.astype(F32)).astype(BF16)
    return hi, mid, lo


def _dot_nt(a, b):
    return lax.dot_general(a, b, (((1,), (1,)), ((), ())), preferred_element_type=F32)


def _dot_nn(a, b):
    return jnp.dot(a, b, preferred_element_type=F32)


def _dot_exact_rhs(x, m_bf16, nt=False):
    d = _dot_nt if nt else _dot_nn
    hi, mid, lo = _split3(x)
    return d(hi, m_bf16) + d(mid, m_bf16) + d(lo, m_bf16)


def _dot_f32_nt(a, b):
    a1, a2, a3 = _split3(a)
    b1, b2, b3 = _split3(b)
    return (_dot_nt(a1, b1) + (_dot_nt(a1, b2) + _dot_nt(a2, b1))
            + (_dot_nt(a1, b3) + _dot_nt(a2, b2) + _dot_nt(a3, b1)))


def _suffix_matrix(n):
    j = lax.broadcasted_iota(jnp.int32, (n, n), 0)
    s = lax.broadcasted_iota(jnp.int32, (n, n), 1)
    return jnp.where(j > s, 1.0, 0.0).astype(BF16)


def _topk_select(gate, nblk, k):
    lane = lax.broadcasted_iota(jnp.int32, gate.shape, 1)
    cnt = jnp.zeros(gate.shape, jnp.int32)
    for m in range(nblk):
        gm = gate[:, m:m + 1]
        ge = (gm >= gate).astype(jnp.int32)
        gt = (gm > gate).astype(jnp.int32)
        cnt = cnt + jnp.where(lane > m, ge, gt)
    return jnp.where(cnt < k, jnp.where(gate > -jnp.inf, 1.0, 0.0), 0.0)


def _params(sem):
    return pltpu.CompilerParams(dimension_semantics=sem, vmem_limit_bytes=VMEM_LIMIT)


def _proj_kernel(*refs, nseg, rope_segs):
    x_ref, g_ref = refs[0], refs[1]
    w_refs = refs[2:2 + nseg]
    pos = 2 + nseg
    if rope_segs:
        c_ref, s1_ref, s2_ref = refs[pos:pos + 3]
        pos += 3
    o_refs = refs[pos:pos + nseg]
    xn = _rmsnorm(x_ref[...], g_ref[...]).astype(BF16)
    for s in range(nseg):
        acc = _dot_nn(xn, w_refs[s][...])
        if s in rope_segs:
            c, s1, s2 = c_ref[...], s1_ref[...], s2_ref[...]
            for hh in range(acc.shape[1] // HEAD_DIM):
                xh = acc[:, hh * HEAD_DIM:(hh + 1) * HEAD_DIM]
                rot = (xh * c + pltpu.roll(xh, ROT_DIM // 2, 1) * s1
                       + pltpu.roll(xh, HEAD_DIM - ROT_DIM // 2, 1) * s2)
                o_refs[s][:, hh * HEAD_DIM:(hh + 1) * HEAD_DIM] = rot
        else:
            o_refs[s][...] = acc


def _project(x, g, w_bf16, nseg, seg_w, rope_segs, rope_tabs, tm, tn, name):
    m, d = x.shape
    nj = seg_w // tn
    ni = m // tm
    in_specs = [pl.BlockSpec((tm, d), lambda j, i: (i, 0)),
                pl.BlockSpec((1, d), lambda j, i: (0, 0))]
    args = [x, g.reshape(1, d)]
    for s in range(nseg):
        in_specs.append(pl.BlockSpec((d, tn), functools.partial(lambda j, i, s: (0, s * nj + j), s=s)))
        args.append(w_bf16)
    if rope_segs:
        nt = rope_tabs[0].shape[0] // tm
        for t in rope_tabs:
            in_specs.append(pl.BlockSpec((tm, HEAD_DIM), lambda j, i: (i % nt, 0)))
            args.append(t)
    return pl.pallas_call(
        functools.partial(_proj_kernel, nseg=nseg, rope_segs=tuple(rope_segs)),
        grid=(nj, ni),
        in_specs=in_specs,
        out_specs=[pl.BlockSpec((tm, tn), lambda j, i: (i, j)) for _ in range(nseg)],
        out_shape=[jax.ShapeDtypeStruct((m, seg_w), F32) for _ in range(nseg)],
        compiler_params=_params(("parallel", "arbitrary")),
        name=name,
    )(*args)


def _gate_kernel(x_ref, g_ref, w_ref, b_ref, o_ref):
    xn = _rmsnorm(x_ref[...], g_ref[...]).astype(BF16)
    o_ref[...] = _log_sigmoid(_dot_nn(xn, w_ref[...]) + b_ref[...])


def _forget_gate(x, g, wf_bf16, bf, tm, name):
    m, d = x.shape
    return pl.pallas_call(
        _gate_kernel,
        grid=(m // tm,),
        in_specs=[pl.BlockSpec((tm, d), lambda i: (i, 0)),
                  pl.BlockSpec((1, d), lambda i: (0, 0)),
                  pl.BlockSpec((d, HEAD_DIM), lambda i: (0, 0)),
                  pl.BlockSpec((1, HEAD_DIM), lambda i: (0, 0))],
        out_specs=pl.BlockSpec((tm, HEAD_DIM), lambda i: (i, 0)),
        out_shape=jax.ShapeDtypeStruct((m, HEAD_DIM), F32),
        compiler_params=_params(("parallel",)),
        name=name,
    )(x, g.reshape(1, d), wf_bf16, bf)


def _outproj_kernel(*refs, n_in):
    o_refs = refs[:n_in]
    w_ref, h_ref, out_ref = refs[n_in:n_in + 3]
    acc = h_ref[...]
    r0 = 0
    for o_ref in o_refs:
        kk = o_ref.shape[1]
        acc = acc + _dot_nn(o_ref[...], w_ref[r0:r0 + kk, :])
        r0 += kk
    out_ref[...] = acc


def _out_project(o_list, w_bf16, h, tm, tn, name):
    m, d = h.shape
    kdim = w_bf16.shape[0]
    in_specs = [pl.BlockSpec((tm, o.shape[1]), lambda j, i: (i, 0)) for o in o_list]
    in_specs += [pl.BlockSpec((kdim, tn), lambda j, i: (0, j)),
                 pl.BlockSpec((tm, tn), lambda j, i: (i, j))]
    return pl.pallas_call(
        functools.partial(_outproj_kernel, n_in=len(o_list)),
        grid=(d // tn, m // tm),
        in_specs=in_specs,
        out_specs=pl.BlockSpec((tm, tn), lambda j, i: (i, j)),
        out_shape=jax.ShapeDtypeStruct((m, d), F32),
        compiler_params=_params(("parallel", "arbitrary")),
        name=name,
    )(*o_list, w_bf16, h)


def _mlp_kernel(*refs, final):
    if final:
        x_ref, g_ref, wu_ref, wd_ref, gf_ref, o_ref, xn_ref = refs
    else:
        x_ref, g_ref, wu_ref, wd_ref, o_ref, xn_ref = refs
    f = pl.program_id(1)

    @pl.when(f == 0)
    def _():
        x = x_ref[...]
        xn_ref[...] = _rmsnorm(x, g_ref[...]).astype(BF16)
        o_ref[...] = x

    u = jnp.maximum(_dot_nn(xn_ref[...], wu_ref[...]), 0.0)
    o_ref[...] += _dot_nn((u * u).astype(BF16), wd_ref[...])

    if final:
        @pl.when(f == pl.num_programs(1) - 1)
        def _():
            o_ref[...] = _rmsnorm(o_ref[...], gf_ref[...])


def _mlp(x, g, wu_bf16, wd_bf16, tm, tf, name, g_final=None):
    m, d = x.shape
    dff = wu_bf16.shape[1]
    final = g_final is not None
    in_specs = [pl.BlockSpec((tm, d), lambda i, f: (i, 0)),
                pl.BlockSpec((1, d), lambda i, f: (0, 0)),
                pl.BlockSpec((d, tf), lambda i, f: (0, f)),
                pl.BlockSpec((tf, d), lambda i, f: (f, 0))]
    args = [x, g.reshape(1, d), wu_bf16, wd_bf16]
    if final:
        in_specs.append(pl.BlockSpec((1, d), lambda i, f: (0, 0)))
        args.append(g_final.reshape(1, d))
    return pl.pallas_call(
        functools.partial(_mlp_kernel, final=final),
        grid=(m // tm, dff // tf),
        in_specs=in_specs,
        out_specs=pl.BlockSpec((tm, d), lambda i, f: (i, 0)),
        out_shape=jax.ShapeDtypeStruct((m, d), F32),
        scratch_shapes=[pltpu.VMEM((tm, d), BF16)],
        compiler_params=_params(("parallel", "arbitrary")),
        name=name,
    )(*args)


def _sb_prompt_kernel(q_ref, k_ref, v_ref, o_ref, *, t):
    qi = pl.program_id(2)
    q = q_ref[...].astype(BF16)
    row = lax.broadcasted_iota(jnp.int32, (t, t), 0)
    col = lax.broadcasted_iota(jnp.int32, (t, t), 1)
    umat = jnp.where(row > col, 1.0, 0.0).astype(BF16)

    def body(j, carry):
        later, acc = carry
        kb = qi - j
        start = pl.multiple_of(kb * t, t)
        k = k_ref[pl.ds(start, t), :].astype(BF16)
        v = v_ref[pl.ds(start, t), :].astype(BF16)
        z = _dot_nt(q, k) * SCALE
        logb = _log_sigmoid(z)
        valid = (col + kb * t) < (row + qi * t)
        lk = jnp.where(valid, logb - z, 0.0)
        rest = _dot_exact_rhs(lk, umat) + later
        a = jnp.where(valid, jnp.exp(logb + rest), 0.0)
        acc = acc + _dot_nn(a.astype(BF16), v)
        later = later + jnp.sum(lk, axis=1, keepdims=True)
        return later, acc

    _, acc = lax.fori_loop(0, qi + 1, body,
                           (jnp.zeros((t, 1), F32), jnp.zeros((t, HEAD_DIM), F32)))
    o_ref[...] = acc.astype(o_ref.dtype)


def _moba_prompt_kernel(q_ref, k_ref, v_ref, o_ref, kmean_ref, *, nblk):
    t = MOBA_BLOCK
    qi = pl.program_id(2)

    @pl.when(qi == 0)
    def _():
        kmean_ref[...] = jnp.zeros(kmean_ref.shape, F32)
        for n in range(nblk):
            kmean_ref[n:n + 1, :] = jnp.mean(k_ref[n * t:(n + 1) * t, :], axis=0, keepdims=True)

    qf = q_ref[...]
    q = qf.astype(BF16)
    lane = lax.broadcasted_iota(jnp.int32, (t, HEAD_DIM), 1)
    gate = jnp.where(lane < qi, _dot_f32_nt(qf, kmean_ref[...]), -jnp.inf)
    sel = _topk_select(gate, nblk, MOBA_TOPK)

    row = lax.broadcasted_iota(jnp.int32, (t, t), 0)
    col = lax.broadcasted_iota(jnp.int32, (t, t), 1)
    start = pl.multiple_of(qi * t, t)
    z = _dot_nt(q, k_ref[pl.ds(start, t), :].astype(BF16)) * SCALE
    z = jnp.where(col <= row, z, NEG_BIG)
    m0 = jnp.max(z, axis=1, keepdims=True)
    p = jnp.exp(z - m0)
    l0 = jnp.sum(p, axis=1, keepdims=True)
    acc0 = _dot_nn(p.astype(BF16), v_ref[pl.ds(start, t), :].astype(BF16))

    def body(n, carry):
        m, l, acc = carry
        chosen = jnp.sum(jnp.where(lane == n, sel, 0.0), axis=1, keepdims=True) > 0.5
        st = pl.multiple_of(n * t, t)
        zz = _dot_nt(q, k_ref[pl.ds(st, t), :].astype(BF16)) * SCALE
        zz = jnp.where(chosen, zz, NEG_BIG)
        m_new = jnp.maximum(m, jnp.max(zz, axis=1, keepdims=True))
        alpha = jnp.exp(m - m_new)
        pp = jnp.exp(zz - m_new)
        l = alpha * l + jnp.sum(pp, axis=1, keepdims=True)
        acc = alpha * acc + _dot_nn(pp.astype(BF16), v_ref[pl.ds(st, t), :].astype(BF16))
        return m_new, l, acc

    _, l, acc = lax.fori_loop(0, qi, body, (m0, l0, acc0))
    o_ref[...] = (acc / l).astype(o_ref.dtype)


def _fox_prompt_kernel(q_ref, k_ref, v_ref, f_ref, o_ref, *, t):
    qi = pl.program_id(2)
    q = q_ref[...].astype(BF16)
    row = lax.broadcasted_iota(jnp.int32, (t, t), 0)
    col = lax.broadcasted_iota(jnp.int32, (t, t), 1)
    start = pl.multiple_of(qi * t, t)
    z = _dot_nt(q, k_ref[pl.ds(start, t), :].astype(BF16)) * SCALE - f_ref[qi]
    z = jnp.where(col <= row, z, NEG_BIG)
    m0 = jnp.max(z, axis=1, keepdims=True)
    p = jnp.exp(z - m0)
    l0 = jnp.sum(p, axis=1, keepdims=True)
    acc0 = _dot_nn(p.astype(BF16), v_ref[pl.ds(start, t), :].astype(BF16))

    def body(n, carry):
        m, l, acc = carry
        st = pl.multiple_of(n * t, t)
        zz = _dot_nt(q, k_ref[pl.ds(st, t), :].astype(BF16)) * SCALE - f_ref[n]
        m_new = jnp.maximum(m, jnp.max(zz, axis=1, keepdims=True))
        alpha = jnp.exp(m - m_new)
        pp = jnp.exp(zz - m_new)
        l = alpha * l + jnp.sum(pp, axis=1, keepdims=True)
        acc = alpha * acc + _dot_nn(pp.astype(BF16), v_ref[pl.ds(st, t), :].astype(BF16))
        return m_new, l, acc

    _, l, acc = lax.fori_loop(0, qi, body, (m0, l0, acc0))
    o_ref[...] = (acc / l).astype(o_ref.dtype)


def _prompt_attention(kind, q, k, v, batch, seq, heads, t, name, fcum=None):
    nq = seq // t
    q_spec = pl.BlockSpec((t, HEAD_DIM), lambda b, h, i: (b * nq + i, h))
    kv_spec = pl.BlockSpec((seq, HEAD_DIM), lambda b, h, i: (b, h))
    in_specs = [q_spec, kv_spec, kv_spec]
    args = [q, k, v]
    scratch = []
    if kind == "sb":
        body = functools.partial(_sb_prompt_kernel, t=t)
    elif kind == "moba":
        body = functools.partial(_moba_prompt_kernel, nblk=seq // MOBA_BLOCK)
        scratch = [pltpu.VMEM((HEAD_DIM, HEAD_DIM), F32)]
    else:
        body = functools.partial(_fox_prompt_kernel, t=t)
        in_specs.append(pl.BlockSpec((None, nq, 1, t), lambda b, h, i: (b * heads + h, 0, 0, 0)))
        args.append(fcum)
    return pl.pallas_call(
        body,
        grid=(batch, heads, nq),
        in_specs=in_specs,
        out_specs=pl.BlockSpec((t, HEAD_DIM), lambda b, h, i: (b * nq + i, h)),
        out_shape=jax.ShapeDtypeStruct((batch * seq, heads * HEAD_DIM), BF16),
        scratch_shapes=scratch,
        compiler_params=_params(("parallel", "parallel", "arbitrary")),
        name=name,
    )(*args)


def _cumsum_kernel(x_ref, o_ref):
    t = MOBA_BLOCK
    r = lax.broadcasted_iota(jnp.int32, (t, t), 0)
    c = lax.broadcasted_iota(jnp.int32, (t, t), 1)
    lower = jnp.where(c <= r, 1.0, 0.0).astype(BF16)
    carry = jnp.zeros((1, x_ref.shape[1]), F32)
    for i in range(x_ref.shape[0] // t):
        hi, mid, lo = _split3(x_ref[i * t:(i + 1) * t, :])
        cs = _dot_nn(lower, hi) + _dot_nn(lower, mid) + _dot_nn(lower, lo) + carry
        carry = cs[t - 1:t, :]
        o_ref[:, i * t:(i + 1) * t] = cs.T[:o_ref.shape[0], :]


def _cumsum_rows_t(x, batch, seq, heads):
    return pl.pallas_call(
        _cumsum_kernel,
        grid=(batch,),
        in_specs=[pl.BlockSpec((seq, HEAD_DIM), lambda b: (b, 0))],
        out_specs=pl.BlockSpec((None, heads, seq), lambda b: (b, 0, 0)),
        out_shape=jax.ShapeDtypeStruct((batch, heads, seq), F32),
        compiler_params=_params(("parallel",)),
        name="fox_cumsum",
    )(x)


def _head_rows(ref, h, heads):
    return ref[pl.ds(h, PAGE, stride=heads), :]


def _page_scores(q_bf16, k_ref, heads):
    rows = lax.broadcasted_iota(jnp.int32, (heads, PAGE), 0)
    zt = jnp.zeros((heads, PAGE), F32)
    for h in range(heads):
        r = _dot_nt(q_bf16, _head_rows(k_ref, h, heads).astype(BF16))
        zt = jnp.where(rows == h, r, zt)
    return zt


def _page_pv(pt, v_ref, heads):
    rows = lax.broadcasted_iota(jnp.int32, (heads, HEAD_DIM), 0)
    p_bf16 = pt.astype(BF16)
    out = jnp.zeros((heads, HEAD_DIM), F32)
    for h in range(heads):
        r = _dot_nn(p_bf16, _head_rows(v_ref, h, heads).astype(BF16))
        out = jnp.where(rows == h, r, out)
    return out


def _sb_sample_kernel(pt_ref, q_ref, k_ref, v_ref, o_ref, later_ref, acc_ref, *, heads):
    del pt_ref
    p = pl.program_id(1)

    @pl.when(p == 0)
    def _():
        later_ref[...] = jnp.zeros(later_ref.shape, F32)
        acc_ref[...] = jnp.zeros(acc_ref.shape, F32)

    z = _page_scores(q_ref[...].astype(BF16), k_ref, heads) * SCALE
    logb = _log_sigmoid(z)
    lk = logb - z
    rest = _dot_exact_rhs(lk, _suffix_matrix(PAGE)) + later_ref[...]
    a = jnp.exp(logb + rest)
    acc_ref[...] += _page_pv(a, v_ref, heads)
    later_ref[...] += jnp.sum(lk, axis=1, keepdims=True)

    @pl.when(p == pl.num_programs(1) - 1)
    def _():
        o_ref[...] = acc_ref[...].astype(o_ref.dtype)


def _sb_sample(q, k_pool, v_pool, page_table, heads):
    nb, npg = page_table.shape
    rows = PAGE * heads
    pool_spec = pl.BlockSpec((rows, HEAD_DIM), lambda b, p, pt: (pt[b * npg + (npg - 1 - p)], 0))
    return pl.pallas_call(
        functools.partial(_sb_sample_kernel, heads=heads),
        grid_spec=pltpu.PrefetchScalarGridSpec(
            num_scalar_prefetch=1,
            grid=(nb, npg),
            in_specs=[pl.BlockSpec((None, heads, HEAD_DIM), lambda b, p, pt: (b, 0, 0)),
                      pool_spec, pool_spec],
            out_specs=pl.BlockSpec((None, heads, HEAD_DIM), lambda b, p, pt: (b, 0, 0)),
            scratch_shapes=[pltpu.VMEM((heads, HEAD_DIM), F32), pltpu.VMEM((heads, HEAD_DIM), F32)]),
        out_shape=jax.ShapeDtypeStruct((nb, heads, HEAD_DIM), BF16),
        compiler_params=_params(("parallel", "arbitrary")),
        name="sb_sample",
    )(page_table.reshape(-1), q, k_pool, v_pool)


def _fox_sample_kernel(pt_ref, q_ref, kn_ref, vn_ref, lfn_ref, k_ref, v_ref, lf_ref, o_ref,
                       m_ref, l_ref, acc_ref, later_ref, *, heads):
    del pt_ref
    p = pl.program_id(1)
    qf = q_ref[...]

    @pl.when(p == 0)
    def _():
        z_new = jnp.sum(qf * kn_ref[...], axis=1, keepdims=True) * SCALE
        m_ref[...] = jnp.broadcast_to(z_new, m_ref.shape)
        l_ref[...] = jnp.ones(l_ref.shape, F32)
        acc_ref[...] = vn_ref[...]
        later_ref[...] = jnp.zeros(later_ref.shape, F32)

    z = _page_scores(qf.astype(BF16), k_ref, heads) * SCALE
    eye = jnp.where(lax.broadcasted_iota(jnp.int32, (heads, heads), 0)
                    == lax.broadcasted_iota(jnp.int32, (heads, heads), 1), 1.0, 0.0).astype(BF16)
    h1, h2, h3 = _split3(lf_ref[...])
    lf_t = _dot_nt(eye, h1) + _dot_nt(eye, h2) + _dot_nt(eye, h3)
    rest = _dot_exact_rhs(lf_t, _suffix_matrix(PAGE)) + later_ref[...]
    zb = z + lfn_ref[...] + rest
    m_old = m_ref[...]
    m_new = jnp.maximum(m_old, jnp.max(zb, axis=1, keepdims=True))
    alpha = jnp.exp(m_old - m_new)
    pp = jnp.exp(zb - m_new)
    l_ref[...] = alpha * l_ref[...] + jnp.sum(pp, axis=1, keepdims=True)
    acc_ref[...] = alpha * acc_ref[...] + _page_pv(pp, v_ref, heads)
    m_ref[...] = m_new
    later_ref[...] += jnp.sum(lf_t, axis=1, keepdims=True)

    @pl.when(p == pl.num_programs(1) - 1)
    def _():
        o_ref[...] = (acc_ref[...] / l_ref[...]).astype(o_ref.dtype)


def _fox_sample(q, k_new, v_new, lf_new, k_pool, v_pool, lf_pool, page_table, heads):
    nb, npg = page_table.shape
    rows = PAGE * heads
    page_of = lambda b, p, pt: pt[b * npg + (npg - 1 - p)]
    tok_spec = pl.BlockSpec((None, heads, HEAD_DIM), lambda b, p, pt: (b, 0, 0))
    pool_spec = pl.BlockSpec((rows, HEAD_DIM), lambda b, p, pt: (page_of(b, p, pt), 0))
    acc = pltpu.VMEM((heads, HEAD_DIM), F32)
    return pl.pallas_call(
        functools.partial(_fox_sample_kernel, heads=heads),
        grid_spec=pltpu.PrefetchScalarGridSpec(
            num_scalar_prefetch=1,
            grid=(nb, npg),
            in_specs=[tok_spec, tok_spec, tok_spec, tok_spec, pool_spec, pool_spec,
                      pl.BlockSpec((None, PAGE, heads), lambda b, p, pt: (page_of(b, p, pt), 0, 0))],
            out_specs=tok_spec,
            scratch_shapes=[acc, acc, acc, acc]),
        out_shape=jax.ShapeDtypeStruct((nb, heads, HEAD_DIM), BF16),
        compiler_params=_params(("parallel", "arbitrary")),
        name="fox_sample",
    )(page_table.reshape(-1), q, k_new, v_new, lf_new, k_pool, v_pool, lf_pool)


def _moba_sample_kernel(pt_ref, q_ref, kn_ref, vn_ref, k_ref, v_ref, o_ref,
                        z_ref, ksum_ref, acc_ref, *, heads, npg):
    del pt_ref
    j = pl.program_id(1)
    qf = q_ref[...]
    ppb = MOBA_BLOCK // PAGE
    nblk = npg // ppb

    @pl.when(j < npg)
    def _():
        z_ref[j] = _page_scores(qf.astype(BF16), k_ref, heads) * SCALE
        ks = jnp.sum(k_ref[...].reshape(PAGE, heads, HEAD_DIM), axis=0)
        blk = j // ppb

        @pl.when(j % ppb == 0)
        def _():
            ksum_ref[blk] = ks

        @pl.when(j % ppb != 0)
        def _():
            ksum_ref[blk] += ks

    @pl.when(j == npg)
    def _():
        lane = lax.broadcasted_iota(jnp.int32, (heads, HEAD_DIM), 1)
        gate = jnp.full((heads, HEAD_DIM), -jnp.inf, F32)
        for n in range(nblk):
            g = jnp.sum(qf * (ksum_ref[n] * (1.0 / MOBA_BLOCK)), axis=1, keepdims=True)
            gate = jnp.where(lane == n, g, gate)
        sel = _topk_select(gate, nblk, MOBA_TOPK)
        z_new = jnp.sum(qf * kn_ref[...], axis=1, keepdims=True) * SCALE
        m = z_new
        chosen = []
        for pg in range(npg):
            c = jnp.sum(jnp.where(lane == pg // ppb, sel, 0.0), axis=1, keepdims=True) > 0.5
            chosen.append(c)
            m = jnp.maximum(m, jnp.max(jnp.where(c, z_ref[pg], NEG_BIG), axis=1, keepdims=True))
        p_new = jnp.exp(z_new - m)
        l = p_new
        for pg in range(npg):
            pp = jnp.exp(jnp.where(chosen[pg], z_ref[pg], NEG_BIG) - m)
            z_ref[pg] = pp
            l = l + jnp.sum(pp, axis=1, keepdims=True)
        inv = 1.0 / l
        for pg in range(npg):
            z_ref[pg] = z_ref[pg] * inv
        acc_ref[...] = (p_new * inv) * vn_ref[...]

    @pl.when(j >= npg)
    def _():
        acc_ref[...] += _page_pv(z_ref[j - npg], v_ref, heads)

    @pl.when(j == 2 * npg - 1)
    def _():
        o_ref[...] = acc_ref[...].astype(o_ref.dtype)


def _moba_sample(q, k_new, v_new, k_pool, v_pool, page_table, heads):
    nb, npg = page_table.shape
    rows = PAGE * heads
    tok_spec = pl.BlockSpec((None, heads, HEAD_DIM), lambda b, j, pt: (b, 0, 0))
    k_spec = pl.BlockSpec((rows, HEAD_DIM), lambda b, j, pt: (pt[b * npg + jnp.minimum(j, npg - 1)], 0))
    v_spec = pl.BlockSpec((rows, HEAD_DIM), lambda b, j, pt: (pt[b * npg + jnp.maximum(j - npg, 0)], 0))
    return pl.pallas_call(
        functools.partial(_moba_sample_kernel, heads=heads, npg=npg),
        grid_spec=pltpu.PrefetchScalarGridSpec(
            num_scalar_prefetch=1,
            grid=(nb, 2 * npg),
            in_specs=[tok_spec, tok_spec, tok_spec, k_spec, v_spec],
            out_specs=tok_spec,
            scratch_shapes=[pltpu.VMEM((npg, heads, PAGE), F32),
                            pltpu.VMEM((npg * PAGE // MOBA_BLOCK, heads, HEAD_DIM), F32),
                            pltpu.VMEM((heads, HEAD_DIM), F32)]),
        out_shape=jax.ShapeDtypeStruct((nb, heads, HEAD_DIM), BF16),
        compiler_params=_params(("parallel", "arbitrary")),
        name="moba_sample",
    )(page_table.reshape(-1), q, k_new, v_new, k_pool, v_pool)


def _rope_tables(pos):
    half = ROT_DIM // 2
    inv = ROPE_THETA ** (-jnp.arange(0, ROT_DIM, 2, dtype=F32) / ROT_DIM)
    ang = pos.astype(F32)[:, None] * inv[None, :]
    cos, sin = jnp.cos(ang), jnp.sin(ang)
    t = pos.shape[0]
    c = jnp.concatenate([cos, cos, jnp.ones((t, HEAD_DIM - ROT_DIM), F32)], axis=1)
    s1 = jnp.concatenate([jnp.zeros((t, half), F32), sin, jnp.zeros((t, HEAD_DIM - ROT_DIM), F32)], axis=1)
    s2 = jnp.concatenate([-sin, jnp.zeros((t, HEAD_DIM - half), F32)], axis=1)
    return c, s1, s2


def _pool2d(pool):
    return pool.reshape(-1, HEAD_DIM)


def kernel(x_prompt, x_sample, cache_k_sb, cache_v_sb, cache_k_moba, cache_v_moba, cache_k_fox, cache_v_fox, cache_logf_fox, page_table, g_mix0, w_in0, w_out0, g_mlp0, w_up0, w_down0, g_mix1, w_in1, b_f1, w_out1, g_mlp1, w_up1, w_down1, g_final):
    bp, seq, d = x_prompt.shape
    bs, tdec, _ = x_sample.shape
    assert tdec == 1
    h_sb = cache_k_sb.shape[2]
    h_mb = cache_k_moba.shape[2]
    h_fox = cache_k_fox.shape[2]
    npg = page_table.shape[1]
    past = npg * PAGE
    mp, ms = bp * seq, bs * tdec
    tm_p = min(512, mp)
    t_att = MOBA_BLOCK

    hp = x_prompt.reshape(mp, d)
    hs = x_sample.reshape(ms, d)

    w_in0_b, w_out0_b = w_in0.astype(BF16), w_out0.astype(BF16)
    w_up0_b, w_down0_b = w_up0.astype(BF16), w_down0.astype(BF16)
    w_in1_b, w_out1_b = w_in1.astype(BF16), w_out1.astype(BF16)
    w_up1_b, w_down1_b = w_up1.astype(BF16), w_down1.astype(BF16)
    nfox = h_fox * HEAD_DIM
    wf_b = jnp.pad(w_in1[:, 3 * nfox:], ((0, 0), (0, HEAD_DIM - h_fox))).astype(BF16)
    bf_pad = jnp.pad(b_f1, (0, HEAD_DIM - h_fox)).reshape(1, HEAD_DIM)

    tabs_p = _rope_tables(jnp.arange(seq, dtype=jnp.int32))
    tabs_s = _rope_tables(jnp.full((ms,), past, jnp.int32))

    seg0 = h_sb * HEAD_DIM
    qa_p, ka_p, va_p, qb_p, kb_p, vb_p = _project(
        hp, g_mix0, w_in0_b, 6, seg0, (3, 4), tabs_p, tm_p, 256, "proj0_prompt")
    qa_s, ka_s, va_s, qb_s, kb_s, vb_s = _project(
        hs, g_mix0, w_in0_b, 6, seg0, (3, 4), tabs_s, ms, 256, "proj0_sample")

    o_sb_p = _prompt_attention("sb", qa_p, ka_p, va_p, bp, seq, h_sb, t_att, "sb_prompt")
    o_mb_p = _prompt_attention("moba", qb_p, kb_p, vb_p, bp, seq, h_mb, MOBA_BLOCK, "moba_prompt")
    hp = _out_project([o_sb_p, o_mb_p], w_out0_b, hp, 1024, 512, "out0_prompt")

    o_sb_s = _sb_sample(qa_s.reshape(bs, h_sb, HEAD_DIM), _pool2d(cache_k_sb), _pool2d(cache_v_sb),
                        page_table, h_sb)
    o_mb_s = _moba_sample(qb_s.reshape(bs, h_mb, HEAD_DIM), kb_s.reshape(bs, h_mb, HEAD_DIM),
                          vb_s.reshape(bs, h_mb, HEAD_DIM), _pool2d(cache_k_moba), _pool2d(cache_v_moba),
                          page_table, h_mb)
    hs = _out_project([o_sb_s.reshape(ms, seg0), o_mb_s.reshape(ms, seg0)], w_out0_b, hs, ms, 512,
                      "out0_sample")

    hp = _mlp(hp, g_mlp0, w_up0_b, w_down0_b, 1024, 512, "mlp0_prompt")
    hs = _mlp(hs, g_mlp0, w_up0_b, w_down0_b, ms, 512, "mlp0_sample")

    q_p, kf_p, vf_p = _project(hp, g_mix1, w_in1_b, 3, nfox, (), None, tm_p, 256, "proj1_prompt")
    q_s, kf_s, vf_s = _project(hs, g_mix1, w_in1_b, 3, nfox, (), None, ms, 256, "proj1_sample")
    lf_p = _forget_gate(hp, g_mix1, wf_b, bf_pad, tm_p, "gate_prompt")
    lf_s = _forget_gate(hs, g_mix1, wf_b, bf_pad, ms, "gate_sample")
    logf_p = lf_p[:, :h_fox]
    logf_s = lf_s[:, :h_fox]

    fcum = _cumsum_rows_t(lf_p, bp, seq, h_fox)
    nq = seq // t_att
    o_fox_p = _prompt_attention("fox", q_p, kf_p, vf_p, bp, seq, h_fox, t_att, "fox_prompt",
                                fcum=fcum.reshape(bp * h_fox, nq, 1, t_att))
    hp = _out_project([o_fox_p], w_out1_b, hp, 1024, 512, "out1_prompt")

    lfn = jnp.broadcast_to(logf_s[:, :, None], (bs, h_fox, HEAD_DIM))
    o_fox_s = _fox_sample(q_s.reshape(bs, h_fox, HEAD_DIM), kf_s.reshape(bs, h_fox, HEAD_DIM),
                          vf_s.reshape(bs, h_fox, HEAD_DIM), lfn, _pool2d(cache_k_fox),
                          _pool2d(cache_v_fox), cache_logf_fox, page_table, h_fox)
    hs = _out_project([o_fox_s.reshape(ms, nfox)], w_out1_b, hs, ms, 512, "out1_sample")

    y_p = _mlp(hp, g_mlp1, w_up1_b, w_down1_b, 1024, 512, "mlp1_prompt", g_final=g_final)
    y_s = _mlp(hs, g_mlp1, w_up1_b, w_down1_b, ms, 512, "mlp1_sample", g_final=g_final)

    r4 = lambda a, b, h: a.reshape(b, -1, h, HEAD_DIM)
    return (y_p.reshape(bp, seq, d), y_s.reshape(bs, tdec, d),
            r4(ka_p, bp, h_sb), r4(ka_s, bs, h_sb), r4(va_p, bp, h_sb), r4(va_s, bs, h_sb),
            r4(kb_p, bp, h_mb), r4(kb_s, bs, h_mb), r4(vb_p, bp, h_mb), r4(vb_s, bs, h_mb),
            r4(kf_p, bp, h_fox), r4(kf_s, bs, h_fox), r4(vf_p, bp, h_fox), r4(vf_s, bs, h_fox),
            logf_p.reshape(bp, seq, h_fox), logf_s.reshape(bs, tdec, h_fox))
```

```python
import functools

import jax
import jax.numpy as jnp
from jax import lax
from jax.experimental import pallas as pl
from jax.experimental.pallas import tpu as pltpu

HEAD_DIM = 128
LANES = 128
ROT_DIM = HEAD_DIM // 4
ROPE_THETA = 500000.0
MOBA_BLOCK = 256
MOBA_TOPK = 3
PAGE = 128
EPS = 1e-6
SCALE = HEAD_DIM ** -0.5
NEG_BIG = -1e30
VMEM_LIMIT = 56 * 1024 * 1024
HEAD_GROUP = 4
PAGES_PER_STEP = 4

F32 = jnp.float32
BF16 = jnp.bfloat16


def _rmsnorm(xf, g):
    ms = jnp.mean(xf * xf, axis=-1, keepdims=True)
    return xf * lax.rsqrt(ms + EPS) * g


def _log_sigmoid(z):
    return jnp.minimum(z, 0.0) - jnp.log1p(jnp.exp(-jnp.abs(z)))


def _split3(x):
    hi = x.astype(BF16)
    r = x - hi.astype(F32)
    mid = r.astype(BF16)
    lo = (r - mid.astype(F32)).astype(BF16)
    return hi, mid, lo


def _dot_nt(a, b):
    return lax.dot_general(a, b, (((1,), (1,)), ((), ())), preferred_element_type=F32)


def _dot_nn(a, b):
    return jnp.dot(a, b, preferred_element_type=F32)


def _dot_exact_rhs(x, m_bf16, pieces=3):
    hi, mid, lo = _split3(x)
    out = _dot_nn(hi, m_bf16) + _dot_nn(mid, m_bf16)
    return out + _dot_nn(lo, m_bf16) if pieces == 3 else out


def _dot_f32_nt(a, b):
    a1, a2, a3 = _split3(a)
    b1, b2, b3 = _split3(b)
    return (_dot_nt(a1, b1) + (_dot_nt(a1, b2) + _dot_nt(a2, b1))
            + (_dot_nt(a1, b3) + _dot_nt(a2, b2) + _dot_nt(a3, b1)))


def _suffix_matrix(n):
    j = lax.broadcasted_iota(jnp.int32, (n, n), 0)
    s = lax.broadcasted_iota(jnp.int32, (n, n), 1)
    return jnp.where(j > s, 1.0, 0.0).astype(BF16)


def _topk_select(gate, nblk, k):
    lane = lax.broadcasted_iota(jnp.int32, gate.shape, 1)
    cnt = jnp.zeros(gate.shape, jnp.int32)
    for m in range(nblk):
        gm = gate[:, m:m + 1]
        ge = (gm >= gate).astype(jnp.int32)
        gt = (gm > gate).astype(jnp.int32)
        cnt = cnt + jnp.where(lane > m, ge, gt)
    return jnp.where(cnt < k, jnp.where(gate > -jnp.inf, 1.0, 0.0), 0.0)


def _online_init(zz, v_bf16):
    m = jnp.max(zz, axis=1, keepdims=True)
    p = jnp.exp(zz - m)
    return m, jnp.sum(p, axis=1, keepdims=True), _dot_nn(p.astype(BF16), v_bf16)


def _online_update(zz, v_bf16, m, l, acc):
    m_new = jnp.maximum(m, jnp.max(zz, axis=1, keepdims=True))
    alpha = jnp.exp(m - m_new)
    p = jnp.exp(zz - m_new)
    l = alpha * l + jnp.sum(p, axis=1, keepdims=True)
    acc = alpha * acc + _dot_nn(p.astype(BF16), v_bf16)
    return m_new, l, acc


def _params(sem):
    return pltpu.CompilerParams(dimension_semantics=sem, vmem_limit_bytes=VMEM_LIMIT)


def _proj_kernel(*refs, nseg, rope_segs):
    x_ref, g_ref = refs[0], refs[1]
    w_refs = refs[2:2 + nseg]
    pos = 2 + nseg
    if rope_segs:
        c_ref, s1_ref, s2_ref = refs[pos:pos + 3]
        pos += 3
    o_refs = refs[pos:pos + nseg]
    xn = _rmsnorm(x_ref[...], g_ref[...]).astype(BF16)
    for s in range(nseg):
        acc = _dot_nn(xn, w_refs[s][...])
        if s in rope_segs:
            c, s1, s2 = c_ref[...], s1_ref[...], s2_ref[...]
            for hh in range(acc.shape[1] // HEAD_DIM):
                xh = acc[:, hh * HEAD_DIM:(hh + 1) * HEAD_DIM]
                rot = (xh * c + pltpu.roll(xh, ROT_DIM // 2, 1) * s1
                       + pltpu.roll(xh, HEAD_DIM - ROT_DIM // 2, 1) * s2)
                o_refs[s][:, hh * HEAD_DIM:(hh + 1) * HEAD_DIM] = rot
        else:
            o_refs[s][...] = acc


def _project(x, g, w_bf16, nseg, seg_w, rope_segs, rope_tabs, tm, tn, name):
    m, d = x.shape
    nj = seg_w // tn
    ni = m // tm
    in_specs = [pl.BlockSpec((tm, d), lambda j, i: (i, 0)),
                pl.BlockSpec((1, d), lambda j, i: (0, 0))]
    args = [x, g.reshape(1, d)]
    for s in range(nseg):
        in_specs.append(pl.BlockSpec((d, tn), functools.partial(lambda j, i, s: (0, s * nj + j), s=s)))
        args.append(w_bf16)
    if rope_segs:
        nt = rope_tabs[0].shape[0] // tm
        for t in rope_tabs:
            in_specs.append(pl.BlockSpec((tm, HEAD_DIM), lambda j, i: (i % nt, 0)))
            args.append(t)
    return pl.pallas_call(
        functools.partial(_proj_kernel, nseg=nseg, rope_segs=tuple(rope_segs)),
        grid=(nj, ni),
        in_specs=in_specs,
        out_specs=[pl.BlockSpec((tm, tn), lambda j, i: (i, j)) for _ in range(nseg)],
        out_shape=[jax.ShapeDtypeStruct((m, seg_w), F32) for _ in range(nseg)],
        compiler_params=_params(("parallel", "arbitrary")),
        name=name,
    )(*args)


def _gate_kernel(x_ref, g_ref, w_ref, b_ref, o_ref):
    xn = _rmsnorm(x_ref[...], g_ref[...]).astype(BF16)
    o_ref[...] = _log_sigmoid(_dot_nn(xn, w_ref[...]) + b_ref[...])


def _forget_gate(x, g, wf_bf16, bf, tm, name):
    m, d = x.shape
    return pl.pallas_call(
        _gate_kernel,
        grid=(m // tm,),
        in_specs=[pl.BlockSpec((tm, d), lambda i: (i, 0)),
                  pl.BlockSpec((1, d), lambda i: (0, 0)),
                  pl.BlockSpec((d, HEAD_DIM), lambda i: (0, 0)),
                  pl.BlockSpec((1, HEAD_DIM), lambda i: (0, 0))],
        out_specs=pl.BlockSpec((tm, HEAD_DIM), lambda i: (i, 0)),
        out_shape=jax.ShapeDtypeStruct((m, HEAD_DIM), F32),
        compiler_params=_params(("parallel",)),
        name=name,
    )(x, g.reshape(1, d), wf_bf16, bf)


def _outproj_kernel(*refs, n_in):
    o_refs = refs[:n_in]
    w_ref, h_ref, out_ref = refs[n_in:n_in + 3]
    acc = h_ref[...]
    r0 = 0
    for o_ref in o_refs:
        kk = o_ref.shape[1]
        acc = acc + _dot_nn(o_ref[...], w_ref[r0:r0 + kk, :])
        r0 += kk
    out_ref[...] = acc


def _out_project(o_list, w_bf16, h, tm, tn, name):
    m, d = h.shape
    kdim = w_bf16.shape[0]
    in_specs = [pl.BlockSpec((tm, o.shape[1]), lambda j, i: (i, 0)) for o in o_list]
    in_specs += [pl.BlockSpec((kdim, tn), lambda j, i: (0, j)),
                 pl.BlockSpec((tm, tn), lambda j, i: (i, j))]
    return pl.pallas_call(
        functools.partial(_outproj_kernel, n_in=len(o_list)),
        grid=(d // tn, m // tm),
        in_specs=in_specs,
        out_specs=pl.BlockSpec((tm, tn), lambda j, i: (i, j)),
        out_shape=jax.ShapeDtypeStruct((m, d), F32),
        compiler_params=_params(("parallel", "arbitrary")),
        name=name,
    )(*o_list, w_bf16, h)


def _mlp_kernel(*refs, final):
    if final:
        x_ref, g_ref, wu_ref, wd_ref, gf_ref, o_ref, xn_ref = refs
    else:
        x_ref, g_ref, wu_ref, wd_ref, o_ref, xn_ref = refs
    f = pl.program_id(1)

    @pl.when(f == 0)
    def _():
        x = x_ref[...]
        xn_ref[...] = _rmsnorm(x, g_ref[...]).astype(BF16)
        o_ref[...] = x

    u = jnp.maximum(_dot_nn(xn_ref[...], wu_ref[...]), 0.0)
    o_ref[...] += _dot_nn((u * u).astype(BF16), wd_ref[...])

    if final:
        @pl.when(f == pl.num_programs(1) - 1)
        def _():
            o_ref[...] = _rmsnorm(o_ref[...], gf_ref[...])


def _mlp(x, g, wu_bf16, wd_bf16, tm, tf, name, g_final=None):
    m, d = x.shape
    dff = wu_bf16.shape[1]
    final = g_final is not None
    in_specs = [pl.BlockSpec((tm, d), lambda i, f: (i, 0)),
                pl.BlockSpec((1, d), lambda i, f: (0, 0)),
                pl.BlockSpec((d, tf), lambda i, f: (0, f)),
                pl.BlockSpec((tf, d), lambda i, f: (f, 0))]
    args = [x, g.reshape(1, d), wu_bf16, wd_bf16]
    if final:
        in_specs.append(pl.BlockSpec((1, d), lambda i, f: (0, 0)))
        args.append(g_final.reshape(1, d))
    return pl.pallas_call(
        functools.partial(_mlp_kernel, final=final),
        grid=(m // tm, dff // tf),
        in_specs=in_specs,
        out_specs=pl.BlockSpec((tm, d), lambda i, f: (i, 0)),
        out_shape=jax.ShapeDtypeStruct((m, d), F32),
        scratch_shapes=[pltpu.VMEM((tm, d), BF16)],
        compiler_params=_params(("parallel", "arbitrary")),
        name=name,
    )(*args)


def _stage_bf16(qi, k_ref, v_ref, kb_ref, vb_ref):
    @pl.when(qi == 0)
    def _():
        kb_ref[...] = k_ref[...].astype(BF16)
        vb_ref[...] = v_ref[...].astype(BF16)


def _head_cols(g):
    return slice(g * HEAD_DIM, (g + 1) * HEAD_DIM)


def _sb_prompt_kernel(q_ref, k_ref, v_ref, o_ref, kb_ref, vb_ref, *, t, group):
    qi = pl.program_id(2)
    _stage_bf16(qi, k_ref, v_ref, kb_ref, vb_ref)
    row = lax.broadcasted_iota(jnp.int32, (t, t), 0)
    col = lax.broadcasted_iota(jnp.int32, (t, t), 1)
    umat = jnp.where(row > col, 1.0, 0.0).astype(BF16)
    strict = col < row
    qs = [q_ref[:, _head_cols(g)].astype(BF16) for g in range(group)]

    def tile(g, kb, later, acc, diagonal):
        start = pl.multiple_of(kb * t, t)
        z = _dot_nt(qs[g], kb_ref[pl.ds(start, t), _head_cols(g)]) * SCALE
        logb = _log_sigmoid(z)
        lk = logb - z
        if diagonal:
            lk = jnp.where(strict, lk, 0.0)
        rest = _dot_exact_rhs(lk, umat, pieces=2) + later
        a = jnp.exp(logb + rest)
        if diagonal:
            a = jnp.where(strict, a, 0.0)
        acc = acc + _dot_nn(a.astype(BF16), vb_ref[pl.ds(start, t), _head_cols(g)])
        return later + jnp.sum(lk, axis=1, keepdims=True), acc

    state = []
    for g in range(group):
        state += list(tile(g, qi, jnp.zeros((t, 1), F32), jnp.zeros((t, HEAD_DIM), F32), True))

    def body(j, carry):
        out = []
        for g in range(group):
            out += list(tile(g, qi - j, carry[2 * g], carry[2 * g + 1], False))
        return tuple(out)

    state = lax.fori_loop(1, qi + 1, body, tuple(state))
    for g in range(group):
        o_ref[:, _head_cols(g)] = state[2 * g + 1].astype(o_ref.dtype)


def _moba_prompt_kernel(q_ref, k_ref, v_ref, o_ref, kb_ref, vb_ref, kmean_ref, *, nblk, group):
    t = MOBA_BLOCK
    qi = pl.program_id(2)
    _stage_bf16(qi, k_ref, v_ref, kb_ref, vb_ref)

    @pl.when(qi == 0)
    def _():
        kmean_ref[...] = jnp.zeros(kmean_ref.shape, F32)
        for g in range(group):
            for n in range(nblk):
                kmean_ref[g, n:n + 1, :] = jnp.mean(k_ref[n * t:(n + 1) * t, _head_cols(g)],
                                                    axis=0, keepdims=True)

    row = lax.broadcasted_iota(jnp.int32, (t, t), 0)
    col = lax.broadcasted_iota(jnp.int32, (t, t), 1)
    causal = col <= row
    lane = lax.broadcasted_iota(jnp.int32, (t, LANES), 1)
    start = pl.multiple_of(qi * t, t)
    qs, sels, state = [], [], []
    for g in range(group):
        qf = q_ref[:, _head_cols(g)]
        q = qf.astype(BF16)
        gate = jnp.where(lane < qi, _dot_f32_nt(qf, kmean_ref[g]), -jnp.inf)
        qs.append(q)
        sels.append(_topk_select(gate, nblk, MOBA_TOPK))
        z = _dot_nt(q, kb_ref[pl.ds(start, t), _head_cols(g)]) * SCALE
        state += list(_online_init(jnp.where(causal, z, NEG_BIG), vb_ref[pl.ds(start, t), _head_cols(g)]))

    def body(n, carry):
        st = pl.multiple_of(n * t, t)
        out = []
        for g in range(group):
            chosen = jnp.sum(jnp.where(lane == n, sels[g], 0.0), axis=1, keepdims=True) > 0.5
            z = _dot_nt(qs[g], kb_ref[pl.ds(st, t), _head_cols(g)]) * SCALE
            out += list(_online_update(jnp.where(chosen, z, NEG_BIG), vb_ref[pl.ds(st, t), _head_cols(g)],
                                       *carry[3 * g:3 * g + 3]))
        return tuple(out)

    state = lax.fori_loop(0, qi, body, tuple(state))
    for g in range(group):
        o_ref[:, _head_cols(g)] = (state[3 * g + 2] / state[3 * g + 1]).astype(o_ref.dtype)


def _fox_prompt_kernel(q_ref, k_ref, v_ref, f_ref, o_ref, kb_ref, vb_ref, *, t, group):
    qi = pl.program_id(2)
    _stage_bf16(qi, k_ref, v_ref, kb_ref, vb_ref)
    row = lax.broadcasted_iota(jnp.int32, (t, t), 0)
    col = lax.broadcasted_iota(jnp.int32, (t, t), 1)
    causal = col <= row
    start = pl.multiple_of(qi * t, t)
    qs, state = [], []
    for g in range(group):
        q = q_ref[:, _head_cols(g)].astype(BF16)
        qs.append(q)
        z = _dot_nt(q, kb_ref[pl.ds(start, t), _head_cols(g)]) * SCALE - f_ref[g, qi]
        state += list(_online_init(jnp.where(causal, z, NEG_BIG), vb_ref[pl.ds(start, t), _head_cols(g)]))

    def body(n, carry):
        st = pl.multiple_of(n * t, t)
        out = []
        for g in range(group):
            z = _dot_nt(qs[g], kb_ref[pl.ds(st, t), _head_cols(g)]) * SCALE - f_ref[g, n]
            out += list(_online_update(z, vb_ref[pl.ds(st, t), _head_cols(g)], *carry[3 * g:3 * g + 3]))
        return tuple(out)

    state = lax.fori_loop(0, qi, body, tuple(state))
    for g in range(group):
        o_ref[:, _head_cols(g)] = (state[3 * g + 2] / state[3 * g + 1]).astype(o_ref.dtype)


def _prompt_attention(kind, q, k, v, batch, seq, heads, t, name, fcum=None):
    nq = seq // t
    group = min(HEAD_GROUP, heads)
    gw = group * HEAD_DIM
    q_spec = pl.BlockSpec((t, gw), lambda b, h, i: (b * nq + i, h))
    kv_spec = pl.BlockSpec((seq, gw), lambda b, h, i: (b, h))
    in_specs = [q_spec, kv_spec, kv_spec]
    args = [q, k, v]
    scratch = [pltpu.VMEM((seq, gw), BF16), pltpu.VMEM((seq, gw), BF16)]
    if kind == "sb":
        body = functools.partial(_sb_prompt_kernel, t=t, group=group)
    elif kind == "moba":
        body = functools.partial(_moba_prompt_kernel, nblk=seq // MOBA_BLOCK, group=group)
        scratch.append(pltpu.VMEM((group, LANES, HEAD_DIM), F32))
    else:
        body = functools.partial(_fox_prompt_kernel, t=t, group=group)
        in_specs.append(pl.BlockSpec((group, nq, 1, t), lambda b, h, i: (b * (heads // group) + h, 0, 0, 0)))
        args.append(fcum)
    return pl.pallas_call(
        body,
        grid=(batch, heads // group, nq),
        in_specs=in_specs,
        out_specs=pl.BlockSpec((t, gw), lambda b, h, i: (b * nq + i, h)),
        out_shape=jax.ShapeDtypeStruct((batch * seq, heads * HEAD_DIM), BF16),
        scratch_shapes=scratch,
        compiler_params=_params(("parallel", "parallel", "arbitrary")),
        name=name,
    )(*args)


def _cumsum_kernel(x_ref, o_ref):
    t = MOBA_BLOCK
    r = lax.broadcasted_iota(jnp.int32, (t, t), 0)
    c = lax.broadcasted_iota(jnp.int32, (t, t), 1)
    lower = jnp.where(c <= r, 1.0, 0.0).astype(BF16)
    carry = jnp.zeros((1, x_ref.shape[1]), F32)
    for i in range(x_ref.shape[0] // t):
        hi, mid, lo = _split3(x_ref[i * t:(i + 1) * t, :])
        cs = _dot_nn(lower, hi) + _dot_nn(lower, mid) + _dot_nn(lower, lo) + carry
        carry = cs[t - 1:t, :]
        o_ref[:, i * t:(i + 1) * t] = cs.T[:o_ref.shape[0], :]


def _cumsum_rows_t(x, batch, seq, heads):
    return pl.pallas_call(
        _cumsum_kernel,
        grid=(batch,),
        in_specs=[pl.BlockSpec((seq, HEAD_DIM), lambda b: (b, 0))],
        out_specs=pl.BlockSpec((None, heads, seq), lambda b: (b, 0, 0)),
        out_shape=jax.ShapeDtypeStruct((batch, heads, seq), F32),
        compiler_params=_params(("parallel",)),
        name="fox_cumsum",
    )(x)


def _diag_mask(heads):
    shape = (heads, PAGE * heads)
    row = lax.broadcasted_iota(jnp.int32, shape, 0)
    lane = lax.broadcasted_iota(jnp.int32, shape, 1)
    return jnp.bitwise_and(lane, heads - 1) == row


def _masked_suffix(xs, later, umat):
    h, width = xs[0].shape
    nc = width // LANES
    chunks = [x[:, c * LANES:(c + 1) * LANES] for x in xs for c in range(nc)]
    within = _dot_exact_rhs(jnp.concatenate(chunks, axis=0), umat)
    carry = later
    rests = []
    for i in range(len(xs)):
        pieces = [None] * nc
        for c in reversed(range(nc)):
            r0 = (i * nc + c) * h
            w = within[r0:r0 + h, :]
            pieces[c] = w + carry
            carry = carry + (w[:, 0:1] + chunks[i * nc + c][:, 0:1])
        rests.append(jnp.concatenate(pieces, axis=1))
    return rests, carry


def _sb_sample_kernel(pt_ref, q_ref, *refs, heads, pps):
    del pt_ref
    k_refs, v_refs = refs[:pps], refs[pps:2 * pps]
    o_ref, later_ref, acc_ref = refs[2 * pps:]
    p = pl.program_id(1)

    @pl.when(p == 0)
    def _():
        later_ref[...] = jnp.zeros(later_ref.shape, F32)
        acc_ref[...] = jnp.zeros(acc_ref.shape, F32)

    q = q_ref[...].astype(BF16)
    diag = _diag_mask(heads)
    umat = _suffix_matrix(LANES)
    zs = [_dot_nt(q, k_refs[i][...].astype(BF16)) * SCALE for i in range(pps)]
    logbs = [_log_sigmoid(z) for z in zs]
    lks = [jnp.where(diag, logb - z, 0.0) for logb, z in zip(logbs, zs)]
    rests, later = _masked_suffix(lks, later_ref[:, 0:1], umat)
    acc = acc_ref[...]
    for i in range(pps):
        a = jnp.where(diag, jnp.exp(logbs[i] + rests[i]), 0.0)
        acc = acc + _dot_nn(a.astype(BF16), v_refs[i][...].astype(BF16))
    later_ref[...] = jnp.broadcast_to(later, later_ref.shape)
    acc_ref[...] = acc

    @pl.when(p == pl.num_programs(1) - 1)
    def _():
        o_ref[...] = acc.astype(o_ref.dtype)


def _newest_first_specs(shape, npg, pps, last_dims):
    zeros = (0,) * last_dims
    return [pl.BlockSpec(shape, functools.partial(
        lambda b, p, pt, i: (pt[b * npg + (npg - 1 - (p * pps + i))],) + zeros, i=i)) for i in range(pps)]


def _sb_sample(q, k_pool, v_pool, page_table, heads):
    nb, npg = page_table.shape
    pps = min(PAGES_PER_STEP, npg)
    pool_specs = _newest_first_specs((PAGE * heads, HEAD_DIM), npg, pps, 1)
    tok_spec = pl.BlockSpec((None, heads, HEAD_DIM), lambda b, p, pt: (b, 0, 0))
    acc = pltpu.VMEM((heads, HEAD_DIM), F32)
    return pl.pallas_call(
        functools.partial(_sb_sample_kernel, heads=heads, pps=pps),
        grid_spec=pltpu.PrefetchScalarGridSpec(
            num_scalar_prefetch=1,
            grid=(nb, npg // pps),
            in_specs=[tok_spec] + pool_specs + pool_specs,
            out_specs=tok_spec,
            scratch_shapes=[acc, acc]),
        out_shape=jax.ShapeDtypeStruct((nb, heads, HEAD_DIM), BF16),
        compiler_params=_params(("parallel", "arbitrary")),
        name="sb_sample",
    )(page_table.reshape(-1), q, *([k_pool] * pps), *([v_pool] * pps))


def _fox_sample_kernel(pt_ref, q_ref, kn_ref, vn_ref, lfn_ref, *refs, heads, pps):
    del pt_ref
    k_refs, v_refs, lf_refs = refs[:pps], refs[pps:2 * pps], refs[2 * pps:3 * pps]
    o_ref, m_ref, l_ref, acc_ref, later_ref = refs[3 * pps:]
    p = pl.program_id(1)
    qf = q_ref[...]

    @pl.when(p == 0)
    def _():
        z_new = jnp.sum(qf * kn_ref[...], axis=1, keepdims=True) * SCALE
        m_ref[...] = jnp.broadcast_to(z_new, m_ref.shape)
        l_ref[...] = jnp.ones(l_ref.shape, F32)
        acc_ref[...] = vn_ref[...]
        later_ref[...] = jnp.zeros(later_ref.shape, F32)

    q = qf.astype(BF16)
    diag = _diag_mask(heads)
    umat = _suffix_matrix(LANES)
    lf_new = lfn_ref[:, 0:1]
    m, l, acc = m_ref[:, 0:1], l_ref[:, 0:1], acc_ref[...]
    zs = [_dot_nt(q, k_refs[i][...].astype(BF16)) * SCALE for i in range(pps)]
    lfs = [jnp.where(diag, lf_refs[i][...], 0.0) for i in range(pps)]
    rests, later = _masked_suffix(lfs, later_ref[:, 0:1], umat)
    zbs = [jnp.where(diag, zs[i] + lf_new + rests[i], NEG_BIG) for i in range(pps)]
    m_new = m
    for zb in zbs:
        m_new = jnp.maximum(m_new, jnp.max(zb, axis=1, keepdims=True))
    alpha = jnp.exp(m - m_new)
    l = alpha * l
    acc = alpha * acc
    for i in range(pps):
        pexp = jnp.exp(zbs[i] - m_new)
        l = l + jnp.sum(pexp, axis=1, keepdims=True)
        acc = acc + _dot_nn(pexp.astype(BF16), v_refs[i][...].astype(BF16))
    m = m_new
    m_ref[...] = jnp.broadcast_to(m, m_ref.shape)
    l_ref[...] = jnp.broadcast_to(l, l_ref.shape)
    later_ref[...] = jnp.broadcast_to(later, later_ref.shape)
    acc_ref[...] = acc

    @pl.when(p == pl.num_programs(1) - 1)
    def _():
        o_ref[...] = (acc / l).astype(o_ref.dtype)


def _fox_sample(q, k_new, v_new, lf_new, k_pool, v_pool, lf_pool, page_table, heads):
    nb, npg = page_table.shape
    pps = min(PAGES_PER_STEP, npg)
    tok_spec = pl.BlockSpec((None, heads, HEAD_DIM), lambda b, p, pt: (b, 0, 0))
    pool_specs = _newest_first_specs((PAGE * heads, HEAD_DIM), npg, pps, 1)
    lf_specs = _newest_first_specs((None, 1, PAGE * heads), npg, pps, 2)
    acc = pltpu.VMEM((heads, HEAD_DIM), F32)
    return pl.pallas_call(
        functools.partial(_fox_sample_kernel, heads=heads, pps=pps),
        grid_spec=pltpu.PrefetchScalarGridSpec(
            num_scalar_prefetch=1,
            grid=(nb, npg // pps),
            in_specs=[tok_spec] * 4 + pool_specs + pool_specs + lf_specs,
            out_specs=tok_spec,
            scratch_shapes=[acc, acc, acc, acc]),
        out_shape=jax.ShapeDtypeStruct((nb, heads, HEAD_DIM), BF16),
        compiler_params=_params(("parallel", "arbitrary")),
        name="fox_sample",
    )(page_table.reshape(-1), q, k_new, v_new, lf_new, *([k_pool] * pps), *([v_pool] * pps),
      *([lf_pool] * pps))


def _moba_sample_kernel(pt_ref, q_ref, kn_ref, vn_ref, *refs, heads, npg, pps):
    del pt_ref
    k_refs, v_refs = refs[:pps], refs[pps:2 * pps]
    o_ref, z_ref, ksum_ref, acc_ref = refs[2 * pps:]
    j = pl.program_id(1)
    nk = npg // pps
    ppb = MOBA_BLOCK // PAGE
    nblk = npg // ppb
    qf = q_ref[...]
    diag = _diag_mask(heads)

    @pl.when(j < nk)
    def _():
        q = qf.astype(BF16)
        sums = []
        for i in range(pps):
            kf = k_refs[i][...]
            z_ref[j * pps + i] = _dot_nt(q, kf.astype(BF16)) * SCALE
            sums.append(jnp.sum(kf.reshape(PAGE, heads, HEAD_DIM), axis=0))
        for blk in range(pps // ppb):
            tot = sums[blk * ppb]
            for i in range(1, ppb):
                tot = tot + sums[blk * ppb + i]
            ksum_ref[j * (pps // ppb) + blk] = tot

    @pl.when(j == nk)
    def _():
        lane = lax.broadcasted_iota(jnp.int32, (heads, LANES), 1)
        gate = jnp.full((heads, LANES), -jnp.inf, F32)
        for n in range(nblk):
            g = jnp.sum(qf * (ksum_ref[n] * (1.0 / MOBA_BLOCK)), axis=1, keepdims=True)
            gate = jnp.where(lane == n, g, gate)
        sel = _topk_select(gate, nblk, MOBA_TOPK)
        chosen = [jnp.sum(jnp.where(lane == n, sel, 0.0), axis=1, keepdims=True) > 0.5
                  for n in range(nblk)]
        z_new = jnp.sum(qf * kn_ref[...], axis=1, keepdims=True) * SCALE
        m = z_new
        for pg in range(npg):
            zz = jnp.where(chosen[pg // ppb], jnp.where(diag, z_ref[pg], NEG_BIG), NEG_BIG)
            z_ref[pg] = zz
            m = jnp.maximum(m, jnp.max(zz, axis=1, keepdims=True))
        p_new = jnp.exp(z_new - m)
        l = p_new
        for pg in range(npg):
            pp = jnp.exp(z_ref[pg] - m)
            z_ref[pg] = pp
            l = l + jnp.sum(pp, axis=1, keepdims=True)
        inv = 1.0 / l
        for pg in range(npg):
            z_ref[pg] = z_ref[pg] * inv
        acc_ref[...] = (p_new * inv) * vn_ref[...]

    @pl.when(j >= nk)
    def _():
        acc = acc_ref[...]
        for i in range(pps):
            acc = acc + _dot_nn(z_ref[(j - nk) * pps + i].astype(BF16), v_refs[i][...].astype(BF16))
        acc_ref[...] = acc

    @pl.when(j == 2 * nk - 1)
    def _():
        o_ref[...] = acc_ref[...].astype(o_ref.dtype)


def _moba_sample(q, k_new, v_new, k_pool, v_pool, page_table, heads):
    nb, npg = page_table.shape
    pps = min(PAGES_PER_STEP, npg)
    assert pps % (MOBA_BLOCK // PAGE) == 0
    nk = npg // pps
    rows = PAGE * heads
    tok_spec = pl.BlockSpec((None, heads, HEAD_DIM), lambda b, j, pt: (b, 0, 0))
    k_specs = [pl.BlockSpec((rows, HEAD_DIM), functools.partial(
        lambda b, j, pt, i: (pt[b * npg + jnp.minimum(j, nk - 1) * pps + i], 0), i=i)) for i in range(pps)]
    v_specs = [pl.BlockSpec((rows, HEAD_DIM), functools.partial(
        lambda b, j, pt, i: (pt[b * npg + jnp.maximum(j - nk, 0) * pps + i], 0), i=i)) for i in range(pps)]
    return pl.pallas_call(
        functools.partial(_moba_sample_kernel, heads=heads, npg=npg, pps=pps),
        grid_spec=pltpu.PrefetchScalarGridSpec(
            num_scalar_prefetch=1,
            grid=(nb, 2 * nk),
            in_specs=[tok_spec] * 3 + k_specs + v_specs,
            out_specs=tok_spec,
            scratch_shapes=[pltpu.VMEM((npg, heads, PAGE * heads), F32),
                            pltpu.VMEM((npg * PAGE // MOBA_BLOCK, heads, HEAD_DIM), F32),
                            pltpu.VMEM((heads, HEAD_DIM), F32)]),
        out_shape=jax.ShapeDtypeStruct((nb, heads, HEAD_DIM), BF16),
        compiler_params=_params(("parallel", "arbitrary")),
        name="moba_sample",
    )(page_table.reshape(-1), q, k_new, v_new, *([k_pool] * pps), *([v_pool] * pps))


def _rope_tables(pos):
    half = ROT_DIM // 2
    inv = ROPE_THETA ** (-jnp.arange(0, ROT_DIM, 2, dtype=F32) / ROT_DIM)
    ang = pos.astype(F32)[:, None] * inv[None, :]
    cos, sin = jnp.cos(ang), jnp.sin(ang)
    t = pos.shape[0]
    c = jnp.concatenate([cos, cos, jnp.ones((t, HEAD_DIM - ROT_DIM), F32)], axis=1)
    s1 = jnp.concatenate([jnp.zeros((t, half), F32), sin, jnp.zeros((t, HEAD_DIM - ROT_DIM), F32)], axis=1)
    s2 = jnp.concatenate([-sin, jnp.zeros((t, HEAD_DIM - half), F32)], axis=1)
    return c, s1, s2


def _pool2d(pool):
    return pool.reshape(-1, HEAD_DIM)


def kernel(x_prompt, x_sample, cache_k_sb, cache_v_sb, cache_k_moba, cache_v_moba, cache_k_fox, cache_v_fox, cache_logf_fox, page_table, g_mix0, w_in0, w_out0, g_mlp0, w_up0, w_down0, g_mix1, w_in1, b_f1, w_out1, g_mlp1, w_up1, w_down1, g_final):
    bp, seq, d = x_prompt.shape
    bs, tdec, _ = x_sample.shape
    assert tdec == 1
    h_sb = cache_k_sb.shape[2]
    h_mb = cache_k_moba.shape[2]
    h_fox = cache_k_fox.shape[2]
    npg = page_table.shape[1]
    past = npg * PAGE
    mp, ms = bp * seq, bs * tdec
    tm_p = min(512, mp)
    t_att = MOBA_BLOCK

    hp = x_prompt.reshape(mp, d)
    hs = x_sample.reshape(ms, d)

    w_in0_b, w_out0_b = w_in0.astype(BF16), w_out0.astype(BF16)
    w_up0_b, w_down0_b = w_up0.astype(BF16), w_down0.astype(BF16)
    w_in1_b, w_out1_b = w_in1.astype(BF16), w_out1.astype(BF16)
    w_up1_b, w_down1_b = w_up1.astype(BF16), w_down1.astype(BF16)
    nfox = h_fox * HEAD_DIM
    wf_b = jnp.pad(w_in1[:, 3 * nfox:], ((0, 0), (0, HEAD_DIM - h_fox))).astype(BF16)
    bf_pad = jnp.pad(b_f1, (0, HEAD_DIM - h_fox)).reshape(1, HEAD_DIM)

    tabs_p = _rope_tables(jnp.arange(seq, dtype=jnp.int32))
    tabs_s = _rope_tables(jnp.full((ms,), past, jnp.int32))

    seg0 = h_sb * HEAD_DIM
    qa_p, ka_p, va_p, qb_p, kb_p, vb_p = _project(
        hp, g_mix0, w_in0_b, 6, seg0, (3, 4), tabs_p, tm_p, 256, "proj0_prompt")
    qa_s, ka_s, va_s, qb_s, kb_s, vb_s = _project(
        hs, g_mix0, w_in0_b, 6, seg0, (3, 4), tabs_s, ms, 256, "proj0_sample")

    o_sb_p = _prompt_attention("sb", qa_p, ka_p, va_p, bp, seq, h_sb, t_att, "sb_prompt")
    o_mb_p = _prompt_attention("moba", qb_p, kb_p, vb_p, bp, seq, h_mb, MOBA_BLOCK, "moba_prompt")
    hp = _out_project([o_sb_p, o_mb_p], w_out0_b, hp, 1024, 512, "out0_prompt")

    o_sb_s = _sb_sample(qa_s.reshape(bs, h_sb, HEAD_DIM), _pool2d(cache_k_sb), _pool2d(cache_v_sb),
                        page_table, h_sb)
    o_mb_s = _moba_sample(qb_s.reshape(bs, h_mb, HEAD_DIM), kb_s.reshape(bs, h_mb, HEAD_DIM),
                          vb_s.reshape(bs, h_mb, HEAD_DIM), _pool2d(cache_k_moba), _pool2d(cache_v_moba),
                          page_table, h_mb)
    hs = _out_project([o_sb_s.reshape(ms, seg0), o_mb_s.reshape(ms, seg0)], w_out0_b, hs, ms, 512,
                      "out0_sample")

    hp = _mlp(hp, g_mlp0, w_up0_b, w_down0_b, 1024, 512, "mlp0_prompt")
    hs = _mlp(hs, g_mlp0, w_up0_b, w_down0_b, ms, 512, "mlp0_sample")

    q_p, kf_p, vf_p = _project(hp, g_mix1, w_in1_b, 3, nfox, (), None, tm_p, 256, "proj1_prompt")
    q_s, kf_s, vf_s = _project(hs, g_mix1, w_in1_b, 3, nfox, (), None, ms, 256, "proj1_sample")
    lf_p = _forget_gate(hp, g_mix1, wf_b, bf_pad, tm_p, "gate_prompt")
    lf_s = _forget_gate(hs, g_mix1, wf_b, bf_pad, ms, "gate_sample")
    logf_p = lf_p[:, :h_fox]
    logf_s = lf_s[:, :h_fox]

    fcum = _cumsum_rows_t(lf_p, bp, seq, h_fox)
    nq = seq // t_att
    o_fox_p = _prompt_attention("fox", q_p, kf_p, vf_p, bp, seq, h_fox, t_att, "fox_prompt",
                                fcum=fcum.reshape(bp * h_fox, nq, 1, t_att))
    hp = _out_project([o_fox_p], w_out1_b, hp, 1024, 512, "out1_prompt")

    lfn = jnp.broadcast_to(logf_s[:, :, None], (bs, h_fox, HEAD_DIM))
    lf_pool = cache_logf_fox.reshape(cache_logf_fox.shape[0], 1, PAGE * h_fox)
    o_fox_s = _fox_sample(q_s.reshape(bs, h_fox, HEAD_DIM), kf_s.reshape(bs, h_fox, HEAD_DIM),
                          vf_s.reshape(bs, h_fox, HEAD_DIM), lfn, _pool2d(cache_k_fox),
                          _pool2d(cache_v_fox), lf_pool, page_table, h_fox)
    hs = _out_project([o_fox_s.reshape(ms, nfox)], w_out1_b, hs, ms, 512, "out1_sample")

    y_p = _mlp(hp, g_mlp1, w_up1_b, w_down1_b, 1024, 512, "mlp1_prompt", g_final=g_final)
    y_s = _mlp(hs, g_mlp1, w_up1_b, w_down1_b, ms, 512, "mlp1_sample", g_final=g_final)

    r4 = lambda a, b, h: a.reshape(b, -1, h, HEAD_DIM)
    return (y_p.reshape(bp, seq, d), y_s.reshape(bs, tdec, d),
            r4(ka_p, bp, h_sb), r4(ka_s, bs, h_sb), r4(va_p, bp, h_sb), r4(va_s, bs, h_sb),
            r4(kb_p, bp, h_mb), r4(kb_s, bs, h_mb), r4(vb_p, bp, h_mb), r4(vb_s, bs, h_mb),
            r4(kf_p, bp, h_fox), r4(kf_s, bs, h_fox), r4(vf_p, bp, h_fox), r4(vf_s, bs, h_fox),
            logf_p.reshape(bp, seq, h_fox), logf_s.reshape(bs, tdec, h_fox))
```

```python
import functools

import jax
import jax.numpy as jnp
from jax import lax
from jax.experimental import pallas as pl
from jax.experimental.pallas import tpu as pltpu

HEAD_DIM = 128
LANES = 128
ROT_DIM = HEAD_DIM // 4
ROPE_THETA = 500000.0
MOBA_BLOCK = 256
MOBA_TOPK = 3
PAGE = 128
EPS = 1e-6
SCALE = HEAD_DIM ** -0.5
NEG_BIG = -1e30
VMEM_LIMIT = 56 * 1024 * 1024
HEAD_GROUP = 4
FOX_TILE = 512
FOX_HEAD_GROUP = 2
PAGES_PER_STEP = 8

F32 = jnp.float32
BF16 = jnp.bfloat16


def _rmsnorm(xf, g):
    ms = jnp.mean(xf * xf, axis=-1, keepdims=True)
    return xf * lax.rsqrt(ms + EPS) * g


def _log_sigmoid(z):
    return jnp.minimum(z, 0.0) - jnp.log1p(jnp.exp(-jnp.abs(z)))


def _split3(x):
    hi = x.astype(BF16)
    r = x - hi.astype(F32)
    mid = r.astype(BF16)
    lo = (r - mid.astype(F32)).astype(BF16)
    return hi, mid, lo


def _dot_nt(a, b):
    return lax.dot_general(a, b, (((1,), (1,)), ((), ())), preferred_element_type=F32)


def _dot_nn(a, b):
    return jnp.dot(a, b, preferred_element_type=F32)


def _dot_exact_rhs(x, m_bf16, pieces=3):
    hi, mid, lo = _split3(x)
    out = _dot_nn(hi, m_bf16) + _dot_nn(mid, m_bf16)
    return out + _dot_nn(lo, m_bf16) if pieces == 3 else out


def _dot_f32_nt(a, b):
    a1, a2, a3 = _split3(a)
    b1, b2, b3 = _split3(b)
    return (_dot_nt(a1, b1) + (_dot_nt(a1, b2) + _dot_nt(a2, b1))
            + (_dot_nt(a1, b3) + _dot_nt(a2, b2) + _dot_nt(a3, b1)))


def _suffix_matrix(n):
    j = lax.broadcasted_iota(jnp.int32, (n, n), 0)
    s = lax.broadcasted_iota(jnp.int32, (n, n), 1)
    return jnp.where(j > s, 1.0, 0.0).astype(BF16)


def _topk_select(gate, nblk, k):
    lane = lax.broadcasted_iota(jnp.int32, gate.shape, 1)
    cnt = jnp.zeros(gate.shape, jnp.int32)
    for m in range(nblk):
        gm = gate[:, m:m + 1]
        ge = (gm >= gate).astype(jnp.int32)
        gt = (gm > gate).astype(jnp.int32)
        cnt = cnt + jnp.where(lane > m, ge, gt)
    return jnp.where(cnt < k, jnp.where(gate > -jnp.inf, 1.0, 0.0), 0.0)


def _online_init(zz, v_bf16):
    m = jnp.max(zz, axis=1, keepdims=True)
    p = jnp.exp(zz - m)
    return m, jnp.sum(p, axis=1, keepdims=True), _dot_nn(p.astype(BF16), v_bf16)


def _online_update(zz, v_bf16, m, l, acc):
    m_new = jnp.maximum(m, jnp.max(zz, axis=1, keepdims=True))
    alpha = jnp.exp(m - m_new)
    p = jnp.exp(zz - m_new)
    l = alpha * l + jnp.sum(p, axis=1, keepdims=True)
    acc = alpha * acc + _dot_nn(p.astype(BF16), v_bf16)
    return m_new, l, acc


def _params(sem):
    return pltpu.CompilerParams(dimension_semantics=sem, vmem_limit_bytes=VMEM_LIMIT)


def _proj_kernel(*refs, nseg, rope_segs):
    x_ref, g_ref = refs[0], refs[1]
    w_refs = refs[2:2 + nseg]
    pos = 2 + nseg
    if rope_segs:
        c_ref, s1_ref, s2_ref = refs[pos:pos + 3]
        pos += 3
    o_refs = refs[pos:pos + nseg]
    xn = _rmsnorm(x_ref[...], g_ref[...]).astype(BF16)
    for s in range(nseg):
        acc = _dot_nn(xn, w_refs[s][...])
        if s in rope_segs:
            c, s1, s2 = c_ref[...], s1_ref[...], s2_ref[...]
            for hh in range(acc.shape[1] // HEAD_DIM):
                xh = acc[:, hh * HEAD_DIM:(hh + 1) * HEAD_DIM]
                rot = (xh * c + pltpu.roll(xh, ROT_DIM // 2, 1) * s1
                       + pltpu.roll(xh, HEAD_DIM - ROT_DIM // 2, 1) * s2)
                o_refs[s][:, hh * HEAD_DIM:(hh + 1) * HEAD_DIM] = rot
        else:
            o_refs[s][...] = acc


def _project(x, g, w_bf16, nseg, seg_w, rope_segs, rope_tabs, tm, tn, name):
    m, d = x.shape
    nj = seg_w // tn
    ni = m // tm
    in_specs = [pl.BlockSpec((tm, d), lambda j, i: (i, 0)),
                pl.BlockSpec((1, d), lambda j, i: (0, 0))]
    args = [x, g.reshape(1, d)]
    for s in range(nseg):
        in_specs.append(pl.BlockSpec((d, tn), functools.partial(lambda j, i, s: (0, s * nj + j), s=s)))
        args.append(w_bf16)
    if rope_segs:
        nt = rope_tabs[0].shape[0] // tm
        for t in rope_tabs:
            in_specs.append(pl.BlockSpec((tm, HEAD_DIM), lambda j, i: (i % nt, 0)))
            args.append(t)
    return pl.pallas_call(
        functools.partial(_proj_kernel, nseg=nseg, rope_segs=tuple(rope_segs)),
        grid=(nj, ni),
        in_specs=in_specs,
        out_specs=[pl.BlockSpec((tm, tn), lambda j, i: (i, j)) for _ in range(nseg)],
        out_shape=[jax.ShapeDtypeStruct((m, seg_w), F32) for _ in range(nseg)],
        compiler_params=_params(("parallel", "arbitrary")),
        name=name,
    )(*args)


def _gate_kernel(x_ref, g_ref, w_ref, b_ref, o_ref):
    xn = _rmsnorm(x_ref[...], g_ref[...]).astype(BF16)
    o_ref[...] = _log_sigmoid(_dot_nn(xn, w_ref[...]) + b_ref[...])


def _forget_gate(x, g, wf_bf16, bf, tm, name):
    m, d = x.shape
    return pl.pallas_call(
        _gate_kernel,
        grid=(m // tm,),
        in_specs=[pl.BlockSpec((tm, d), lambda i: (i, 0)),
                  pl.BlockSpec((1, d), lambda i: (0, 0)),
                  pl.BlockSpec((d, HEAD_DIM), lambda i: (0, 0)),
                  pl.BlockSpec((1, HEAD_DIM), lambda i: (0, 0))],
        out_specs=pl.BlockSpec((tm, HEAD_DIM), lambda i: (i, 0)),
        out_shape=jax.ShapeDtypeStruct((m, HEAD_DIM), F32),
        compiler_params=_params(("parallel",)),
        name=name,
    )(x, g.reshape(1, d), wf_bf16, bf)


def _outproj_kernel(*refs, n_in):
    o_refs = refs[:n_in]
    w_ref, h_ref, out_ref = refs[n_in:n_in + 3]
    acc = h_ref[...]
    r0 = 0
    for o_ref in o_refs:
        kk = o_ref.shape[1]
        acc = acc + _dot_nn(o_ref[...], w_ref[r0:r0 + kk, :])
        r0 += kk
    out_ref[...] = acc


def _out_project(o_list, w_bf16, h, tm, tn, name):
    m, d = h.shape
    kdim = w_bf16.shape[0]
    in_specs = [pl.BlockSpec((tm, o.shape[1]), lambda j, i: (i, 0)) for o in o_list]
    in_specs += [pl.BlockSpec((kdim, tn), lambda j, i: (0, j)),
                 pl.BlockSpec((tm, tn), lambda j, i: (i, j))]
    return pl.pallas_call(
        functools.partial(_outproj_kernel, n_in=len(o_list)),
        grid=(d // tn, m // tm),
        in_specs=in_specs,
        out_specs=pl.BlockSpec((tm, tn), lambda j, i: (i, j)),
        out_shape=jax.ShapeDtypeStruct((m, d), F32),
        compiler_params=_params(("parallel", "arbitrary")),
        name=name,
    )(*o_list, w_bf16, h)


def _mlp_kernel(*refs, final):
    if final:
        x_ref, g_ref, wu_ref, wd_ref, gf_ref, o_ref, xn_ref = refs
    else:
        x_ref, g_ref, wu_ref, wd_ref, o_ref, xn_ref = refs
    f = pl.program_id(1)

    @pl.when(f == 0)
    def _():
        x = x_ref[...]
        xn_ref[...] = _rmsnorm(x, g_ref[...]).astype(BF16)
        o_ref[...] = x

    u = jnp.maximum(_dot_nn(xn_ref[...], wu_ref[...]), 0.0)
    o_ref[...] += _dot_nn((u * u).astype(BF16), wd_ref[...])

    if final:
        @pl.when(f == pl.num_programs(1) - 1)
        def _():
            o_ref[...] = _rmsnorm(o_ref[...], gf_ref[...])


def _mlp(x, g, wu_bf16, wd_bf16, tm, tf, name, g_final=None):
    m, d = x.shape
    dff = wu_bf16.shape[1]
    final = g_final is not None
    in_specs = [pl.BlockSpec((tm, d), lambda i, f: (i, 0)),
                pl.BlockSpec((1, d), lambda i, f: (0, 0)),
                pl.BlockSpec((d, tf), lambda i, f: (0, f)),
                pl.BlockSpec((tf, d), lambda i, f: (f, 0))]
    args = [x, g.reshape(1, d), wu_bf16, wd_bf16]
    if final:
        in_specs.append(pl.BlockSpec((1, d), lambda i, f: (0, 0)))
        args.append(g_final.reshape(1, d))
    return pl.pallas_call(
        functools.partial(_mlp_kernel, final=final),
        grid=(m // tm, dff // tf),
        in_specs=in_specs,
        out_specs=pl.BlockSpec((tm, d), lambda i, f: (i, 0)),
        out_shape=jax.ShapeDtypeStruct((m, d), F32),
        scratch_shapes=[pltpu.VMEM((tm, d), BF16)],
        compiler_params=_params(("parallel", "arbitrary")),
        name=name,
    )(*args)


def _stage_bf16(qi, k_ref, v_ref, kb_ref, vb_ref):
    @pl.when(qi == 0)
    def _():
        kb_ref[...] = k_ref[...].astype(BF16)
        vb_ref[...] = v_ref[...].astype(BF16)


def _head_cols(g):
    return slice(g * HEAD_DIM, (g + 1) * HEAD_DIM)


def _sb_prompt_kernel(q_ref, k_ref, v_ref, o_ref, kb_ref, vb_ref, *, t, group):
    qi = pl.program_id(2)
    _stage_bf16(qi, k_ref, v_ref, kb_ref, vb_ref)
    row = lax.broadcasted_iota(jnp.int32, (t, t), 0)
    col = lax.broadcasted_iota(jnp.int32, (t, t), 1)
    umat = jnp.where(row > col, 1.0, 0.0).astype(BF16)
    strict = col < row
    qs = [q_ref[:, _head_cols(g)].astype(BF16) for g in range(group)]

    def tile(g, kb, later, acc, diagonal):
        start = pl.multiple_of(kb * t, t)
        z = _dot_nt(qs[g], kb_ref[pl.ds(start, t), _head_cols(g)]) * SCALE
        logb = _log_sigmoid(z)
        lk = logb - z
        if diagonal:
            lk = jnp.where(strict, lk, 0.0)
        rest = _dot_exact_rhs(lk, umat, pieces=2) + later
        a = jnp.exp(logb + rest)
        if diagonal:
            a = jnp.where(strict, a, 0.0)
        acc = acc + _dot_nn(a.astype(BF16), vb_ref[pl.ds(start, t), _head_cols(g)])
        return later + jnp.sum(lk, axis=1, keepdims=True), acc

    state = []
    for g in range(group):
        state += list(tile(g, qi, jnp.zeros((t, 1), F32), jnp.zeros((t, HEAD_DIM), F32), True))

    def body(j, carry):
        out = []
        for g in range(group):
            out += list(tile(g, qi - j, carry[2 * g], carry[2 * g + 1], False))
        return tuple(out)

    state = lax.fori_loop(1, qi + 1, body, tuple(state))
    for g in range(group):
        o_ref[:, _head_cols(g)] = state[2 * g + 1].astype(o_ref.dtype)


def _moba_prompt_kernel(q_ref, k_ref, v_ref, o_ref, kb_ref, vb_ref, kmean_ref, *, nblk, group):
    t = MOBA_BLOCK
    qi = pl.program_id(2)
    _stage_bf16(qi, k_ref, v_ref, kb_ref, vb_ref)

    @pl.when(qi == 0)
    def _():
        kmean_ref[...] = jnp.zeros(kmean_ref.shape, F32)
        for g in range(group):
            for n in range(nblk):
                kmean_ref[g, n:n + 1, :] = jnp.mean(k_ref[n * t:(n + 1) * t, _head_cols(g)],
                                                    axis=0, keepdims=True)

    row = lax.broadcasted_iota(jnp.int32, (t, t), 0)
    col = lax.broadcasted_iota(jnp.int32, (t, t), 1)
    causal = col <= row
    lane = lax.broadcasted_iota(jnp.int32, (t, LANES), 1)
    start = pl.multiple_of(qi * t, t)
    qs, sels, state = [], [], []
    for g in range(group):
        qf = q_ref[:, _head_cols(g)]
        q = qf.astype(BF16)
        gate = jnp.where(lane < qi, _dot_f32_nt(qf, kmean_ref[g]), -jnp.inf)
        qs.append(q)
        sels.append(_topk_select(gate, nblk, MOBA_TOPK))
        z = _dot_nt(q, kb_ref[pl.ds(start, t), _head_cols(g)]) * SCALE
        state += list(_online_init(jnp.where(causal, z, NEG_BIG), vb_ref[pl.ds(start, t), _head_cols(g)]))

    def body(n, carry):
        st = pl.multiple_of(n * t, t)
        out = []
        for g in range(group):
            chosen = jnp.sum(jnp.where(lane == n, sels[g], 0.0), axis=1, keepdims=True) > 0.5
            z = _dot_nt(qs[g], kb_ref[pl.ds(st, t), _head_cols(g)]) * SCALE
            out += list(_online_update(jnp.where(chosen, z, NEG_BIG), vb_ref[pl.ds(st, t), _head_cols(g)],
                                       *carry[3 * g:3 * g + 3]))
        return tuple(out)

    state = lax.fori_loop(0, qi, body, tuple(state))
    for g in range(group):
        o_ref[:, _head_cols(g)] = (state[3 * g + 2] / state[3 * g + 1]).astype(o_ref.dtype)


def _fox_prompt_kernel(q_ref, k_ref, v_ref, f_ref, o_ref, kb_ref, vb_ref, *, t, group):
    qi = pl.program_id(2)
    _stage_bf16(qi, k_ref, v_ref, kb_ref, vb_ref)
    row = lax.broadcasted_iota(jnp.int32, (t, t), 0)
    col = lax.broadcasted_iota(jnp.int32, (t, t), 1)
    causal = col <= row
    start = pl.multiple_of(qi * t, t)
    qs, state = [], []
    for g in range(group):
        q = q_ref[:, _head_cols(g)].astype(BF16)
        qs.append(q)
        z = _dot_nt(q, kb_ref[pl.ds(start, t), _head_cols(g)]) * SCALE - f_ref[g, qi]
        state += list(_online_init(jnp.where(causal, z, NEG_BIG), vb_ref[pl.ds(start, t), _head_cols(g)]))

    def body(n, carry):
        st = pl.multiple_of(n * t, t)
        out = []
        for g in range(group):
            z = _dot_nt(qs[g], kb_ref[pl.ds(st, t), _head_cols(g)]) * SCALE - f_ref[g, n]
            out += list(_online_update(z, vb_ref[pl.ds(st, t), _head_cols(g)], *carry[3 * g:3 * g + 3]))
        return tuple(out)

    state = lax.fori_loop(0, qi, body, tuple(state))
    for g in range(group):
        o_ref[:, _head_cols(g)] = (state[3 * g + 2] / state[3 * g + 1]).astype(o_ref.dtype)


def _prompt_attention(kind, q, k, v, batch, seq, heads, t, group, name, fcum=None):
    nq = seq // t
    group = min(group, heads)
    gw = group * HEAD_DIM
    q_spec = pl.BlockSpec((t, gw), lambda b, h, i: (b * nq + i, h))
    kv_spec = pl.BlockSpec((seq, gw), lambda b, h, i: (b, h))
    in_specs = [q_spec, kv_spec, kv_spec]
    args = [q, k, v]
    scratch = [pltpu.VMEM((seq, gw), BF16), pltpu.VMEM((seq, gw), BF16)]
    if kind == "sb":
        body = functools.partial(_sb_prompt_kernel, t=t, group=group)
    elif kind == "moba":
        body = functools.partial(_moba_prompt_kernel, nblk=seq // MOBA_BLOCK, group=group)
        scratch.append(pltpu.VMEM((group, LANES, HEAD_DIM), F32))
    else:
        body = functools.partial(_fox_prompt_kernel, t=t, group=group)
        in_specs.append(pl.BlockSpec((group, nq, 1, t), lambda b, h, i: (b * (heads // group) + h, 0, 0, 0)))
        args.append(fcum)
    return pl.pallas_call(
        body,
        grid=(batch, heads // group, nq),
        in_specs=in_specs,
        out_specs=pl.BlockSpec((t, gw), lambda b, h, i: (b * nq + i, h)),
        out_shape=jax.ShapeDtypeStruct((batch * seq, heads * HEAD_DIM), BF16),
        scratch_shapes=scratch,
        compiler_params=_params(("parallel", "parallel", "arbitrary")),
        name=name,
    )(*args)


def _cumsum_kernel(x_ref, o_ref):
    t = MOBA_BLOCK
    r = lax.broadcasted_iota(jnp.int32, (t, t), 0)
    c = lax.broadcasted_iota(jnp.int32, (t, t), 1)
    lower = jnp.where(c <= r, 1.0, 0.0).astype(BF16)
    carry = jnp.zeros((1, x_ref.shape[1]), F32)
    for i in range(x_ref.shape[0] // t):
        hi, mid, lo = _split3(x_ref[i * t:(i + 1) * t, :])
        cs = _dot_nn(lower, hi) + _dot_nn(lower, mid) + _dot_nn(lower, lo) + carry
        carry = cs[t - 1:t, :]
        o_ref[:, i * t:(i + 1) * t] = cs.T[:o_ref.shape[0], :]


def _cumsum_rows_t(x, batch, seq, heads):
    return pl.pallas_call(
        _cumsum_kernel,
        grid=(batch,),
        in_specs=[pl.BlockSpec((seq, HEAD_DIM), lambda b: (b, 0))],
        out_specs=pl.BlockSpec((None, heads, seq), lambda b: (b, 0, 0)),
        out_shape=jax.ShapeDtypeStruct((batch, heads, seq), F32),
        compiler_params=_params(("parallel",)),
        name="fox_cumsum",
    )(x)


def _diag_mask(heads):
    shape = (heads, PAGE * heads)
    row = lax.broadcasted_iota(jnp.int32, shape, 0)
    lane = lax.broadcasted_iota(jnp.int32, shape, 1)
    return jnp.bitwise_and(lane, heads - 1) == row


def _masked_suffix(xs, later, umat):
    h, width = xs[0].shape
    nc = width // LANES
    chunks = [x[:, c * LANES:(c + 1) * LANES] for x in xs for c in range(nc)]
    within = _dot_exact_rhs(jnp.concatenate(chunks, axis=0), umat)
    carry = later
    rests = []
    for i in range(len(xs)):
        pieces = [None] * nc
        for c in reversed(range(nc)):
            r0 = (i * nc + c) * h
            w = within[r0:r0 + h, :]
            pieces[c] = w + carry
            carry = carry + (w[:, 0:1] + chunks[i * nc + c][:, 0:1])
        rests.append(jnp.concatenate(pieces, axis=1))
    return rests, carry


def _sb_sample_kernel(pt_ref, q_ref, *refs, heads, pps):
    del pt_ref
    k_refs, v_refs = refs[:pps], refs[pps:2 * pps]
    o_ref, later_ref, acc_ref = refs[2 * pps:]
    p = pl.program_id(1)

    @pl.when(p == 0)
    def _():
        later_ref[...] = jnp.zeros(later_ref.shape, F32)
        acc_ref[...] = jnp.zeros(acc_ref.shape, F32)

    q = q_ref[...].astype(BF16)
    diag = _diag_mask(heads)
    umat = _suffix_matrix(LANES)
    zs = [_dot_nt(q, k_refs[i][...].astype(BF16)) * SCALE for i in range(pps)]
    logbs = [_log_sigmoid(z) for z in zs]
    lks = [jnp.where(diag, logb - z, 0.0) for logb, z in zip(logbs, zs)]
    rests, later = _masked_suffix(lks, later_ref[:, 0:1], umat)
    acc = acc_ref[...]
    for i in range(pps):
        a = jnp.where(diag, jnp.exp(logbs[i] + rests[i]), 0.0)
        acc = acc + _dot_nn(a.astype(BF16), v_refs[i][...].astype(BF16))
    later_ref[...] = jnp.broadcast_to(later, later_ref.shape)
    acc_ref[...] = acc

    @pl.when(p == pl.num_programs(1) - 1)
    def _():
        o_ref[...] = acc.astype(o_ref.dtype)


def _newest_first_specs(shape, npg, pps, last_dims):
    zeros = (0,) * last_dims
    return [pl.BlockSpec(shape, functools.partial(
        lambda b, p, pt, i: (pt[b * npg + (npg - 1 - (p * pps + i))],) + zeros, i=i)) for i in range(pps)]


def _sb_sample(q, k_pool, v_pool, page_table, heads):
    nb, npg = page_table.shape
    pps = min(PAGES_PER_STEP, npg)
    pool_specs = _newest_first_specs((PAGE * heads, HEAD_DIM), npg, pps, 1)
    tok_spec = pl.BlockSpec((None, heads, HEAD_DIM), lambda b, p, pt: (b, 0, 0))
    acc = pltpu.VMEM((heads, HEAD_DIM), F32)
    return pl.pallas_call(
        functools.partial(_sb_sample_kernel, heads=heads, pps=pps),
        grid_spec=pltpu.PrefetchScalarGridSpec(
            num_scalar_prefetch=1,
            grid=(nb, npg // pps),
            in_specs=[tok_spec] + pool_specs + pool_specs,
            out_specs=tok_spec,
            scratch_shapes=[acc, acc]),
        out_shape=jax.ShapeDtypeStruct((nb, heads, HEAD_DIM), BF16),
        compiler_params=_params(("parallel", "arbitrary")),
        name="sb_sample",
    )(page_table.reshape(-1), q, *([k_pool] * pps), *([v_pool] * pps))


def _fox_sample_kernel(pt_ref, q_ref, kn_ref, vn_ref, lfn_ref, *refs, heads, pps):
    del pt_ref
    k_refs, v_refs, lf_refs = refs[:pps], refs[pps:2 * pps], refs[2 * pps:3 * pps]
    o_ref, m_ref, l_ref, acc_ref, later_ref = refs[3 * pps:]
    p = pl.program_id(1)
    qf = q_ref[...]

    @pl.when(p == 0)
    def _():
        z_new = jnp.sum(qf * kn_ref[...], axis=1, keepdims=True) * SCALE
        m_ref[...] = jnp.broadcast_to(z_new, m_ref.shape)
        l_ref[...] = jnp.ones(l_ref.shape, F32)
        acc_ref[...] = vn_ref[...]
        later_ref[...] = jnp.zeros(later_ref.shape, F32)

    q = qf.astype(BF16)
    diag = _diag_mask(heads)
    umat = _suffix_matrix(LANES)
    lf_new = lfn_ref[:, 0:1]
    m, l, acc = m_ref[:, 0:1], l_ref[:, 0:1], acc_ref[...]
    zs = [_dot_nt(q, k_refs[i][...].astype(BF16)) * SCALE for i in range(pps)]
    lfs = [jnp.where(diag, lf_refs[i][...], 0.0) for i in range(pps)]
    rests, later = _masked_suffix(lfs, later_ref[:, 0:1], umat)
    zbs = [jnp.where(diag, zs[i] + lf_new + rests[i], NEG_BIG) for i in range(pps)]
    m_new = m
    for zb in zbs:
        m_new = jnp.maximum(m_new, jnp.max(zb, axis=1, keepdims=True))
    alpha = jnp.exp(m - m_new)
    l = alpha * l
    acc = alpha * acc
    for i in range(pps):
        pexp = jnp.exp(zbs[i] - m_new)
        l = l + jnp.sum(pexp, axis=1, keepdims=True)
        acc = acc + _dot_nn(pexp.astype(BF16), v_refs[i][...].astype(BF16))
    m = m_new
    m_ref[...] = jnp.broadcast_to(m, m_ref.shape)
    l_ref[...] = jnp.broadcast_to(l, l_ref.shape)
    later_ref[...] = jnp.broadcast_to(later, later_ref.shape)
    acc_ref[...] = acc

    @pl.when(p == pl.num_programs(1) - 1)
    def _():
        o_ref[...] = (acc / l).astype(o_ref.dtype)


def _fox_sample(q, k_new, v_new, lf_new, k_pool, v_pool, lf_pool, page_table, heads):
    nb, npg = page_table.shape
    pps = min(PAGES_PER_STEP, npg)
    tok_spec = pl.BlockSpec((None, heads, HEAD_DIM), lambda b, p, pt: (b, 0, 0))
    pool_specs = _newest_first_specs((PAGE * heads, HEAD_DIM), npg, pps, 1)
    lf_specs = _newest_first_specs((None, 1, PAGE * heads), npg, pps, 2)
    acc = pltpu.VMEM((heads, HEAD_DIM), F32)
    return pl.pallas_call(
        functools.partial(_fox_sample_kernel, heads=heads, pps=pps),
        grid_spec=pltpu.PrefetchScalarGridSpec(
            num_scalar_prefetch=1,
            grid=(nb, npg // pps),
            in_specs=[tok_spec] * 4 + pool_specs + pool_specs + lf_specs,
            out_specs=tok_spec,
            scratch_shapes=[acc, acc, acc, acc]),
        out_shape=jax.ShapeDtypeStruct((nb, heads, HEAD_DIM), BF16),
        compiler_params=_params(("parallel", "arbitrary")),
        name="fox_sample",
    )(page_table.reshape(-1), q, k_new, v_new, lf_new, *([k_pool] * pps), *([v_pool] * pps),
      *([lf_pool] * pps))


def _moba_sample_kernel(pt_ref, q_ref, kn_ref, vn_ref, *refs, heads, npg, pps):
    del pt_ref
    k_refs, v_refs = refs[:pps], refs[pps:2 * pps]
    o_ref, z_ref, ksum_ref, acc_ref = refs[2 * pps:]
    j = pl.program_id(1)
    nk = npg // pps
    ppb = MOBA_BLOCK // PAGE
    nblk = npg // ppb
    qf = q_ref[...]
    diag = _diag_mask(heads)

    @pl.when(j < nk)
    def _():
        q = qf.astype(BF16)
        sums = []
        for i in range(pps):
            kf = k_refs[i][...]
            z_ref[j * pps + i] = _dot_nt(q, kf.astype(BF16)) * SCALE
            sums.append(jnp.sum(kf.reshape(PAGE, heads, HEAD_DIM), axis=0))
        for blk in range(pps // ppb):
            tot = sums[blk * ppb]
            for i in range(1, ppb):
                tot = tot + sums[blk * ppb + i]
            ksum_ref[j * (pps // ppb) + blk] = tot

    @pl.when(j == nk)
    def _():
        lane = lax.broadcasted_iota(jnp.int32, (heads, LANES), 1)
        gate = jnp.full((heads, LANES), -jnp.inf, F32)
        for n in range(nblk):
            g = jnp.sum(qf * (ksum_ref[n] * (1.0 / MOBA_BLOCK)), axis=1, keepdims=True)
            gate = jnp.where(lane == n, g, gate)
        sel = _topk_select(gate, nblk, MOBA_TOPK)
        chosen = [jnp.sum(jnp.where(lane == n, sel, 0.0), axis=1, keepdims=True) > 0.5
                  for n in range(nblk)]
        z_new = jnp.sum(qf * kn_ref[...], axis=1, keepdims=True) * SCALE
        m = z_new
        for pg in range(npg):
            zz = jnp.where(chosen[pg // ppb], jnp.where(diag, z_ref[pg], NEG_BIG), NEG_BIG)
            z_ref[pg] = zz
            m = jnp.maximum(m, jnp.max(zz, axis=1, keepdims=True))
        p_new = jnp.exp(z_new - m)
        l = p_new
        for pg in range(npg):
            pp = jnp.exp(z_ref[pg] - m)
            z_ref[pg] = pp
            l = l + jnp.sum(pp, axis=1, keepdims=True)
        inv = 1.0 / l
        for pg in range(npg):
            z_ref[pg] = z_ref[pg] * inv
        acc_ref[...] = (p_new * inv) * vn_ref[...]

    @pl.when(j >= nk)
    def _():
        acc = acc_ref[...]
        for i in range(pps):
            acc = acc + _dot_nn(z_ref[(j - nk) * pps + i].astype(BF16), v_refs[i][...].astype(BF16))
        acc_ref[...] = acc

    @pl.when(j == 2 * nk - 1)
    def _():
        o_ref[...] = acc_ref[...].astype(o_ref.dtype)


def _moba_sample(q, k_new, v_new, k_pool, v_pool, page_table, heads):
    nb, npg = page_table.shape
    pps = min(PAGES_PER_STEP, npg)
    assert pps % (MOBA_BLOCK // PAGE) == 0
    nk = npg // pps
    rows = PAGE * heads
    tok_spec = pl.BlockSpec((None, heads, HEAD_DIM), lambda b, j, pt: (b, 0, 0))
    k_specs = [pl.BlockSpec((rows, HEAD_DIM), functools.partial(
        lambda b, j, pt, i: (pt[b * npg + jnp.minimum(j, nk - 1) * pps + i], 0), i=i)) for i in range(pps)]
    v_specs = [pl.BlockSpec((rows, HEAD_DIM), functools.partial(
        lambda b, j, pt, i: (pt[b * npg + jnp.maximum(j - nk, 0) * pps + i], 0), i=i)) for i in range(pps)]
    return pl.pallas_call(
        functools.partial(_moba_sample_kernel, heads=heads, npg=npg, pps=pps),
        grid_spec=pltpu.PrefetchScalarGridSpec(
            num_scalar_prefetch=1,
            grid=(nb, 2 * nk),
            in_specs=[tok_spec] * 3 + k_specs + v_specs,
            out_specs=tok_spec,
            scratch_shapes=[pltpu.VMEM((npg, heads, PAGE * heads), F32),
                            pltpu.VMEM((npg * PAGE // MOBA_BLOCK, heads, HEAD_DIM), F32),
                            pltpu.VMEM((heads, HEAD_DIM), F32)]),
        out_shape=jax.ShapeDtypeStruct((nb, heads, HEAD_DIM), BF16),
        compiler_params=_params(("parallel", "arbitrary")),
        name="moba_sample",
    )(page_table.reshape(-1), q, k_new, v_new, *([k_pool] * pps), *([v_pool] * pps))


def _rope_tables(pos):
    half = ROT_DIM // 2
    inv = ROPE_THETA ** (-jnp.arange(0, ROT_DIM, 2, dtype=F32) / ROT_DIM)
    ang = pos.astype(F32)[:, None] * inv[None, :]
    cos, sin = jnp.cos(ang), jnp.sin(ang)
    t = pos.shape[0]
    c = jnp.concatenate([cos, cos, jnp.ones((t, HEAD_DIM - ROT_DIM), F32)], axis=1)
    s1 = jnp.concatenate([jnp.zeros((t, half), F32), sin, jnp.zeros((t, HEAD_DIM - ROT_DIM), F32)], axis=1)
    s2 = jnp.concatenate([-sin, jnp.zeros((t, HEAD_DIM - half), F32)], axis=1)
    return c, s1, s2


def _pool2d(pool):
    return pool.reshape(-1, HEAD_DIM)


def kernel(x_prompt, x_sample, cache_k_sb, cache_v_sb, cache_k_moba, cache_v_moba, cache_k_fox, cache_v_fox, cache_logf_fox, page_table, g_mix0, w_in0, w_out0, g_mlp0, w_up0, w_down0, g_mix1, w_in1, b_f1, w_out1, g_mlp1, w_up1, w_down1, g_final):
    bp, seq, d = x_prompt.shape
    bs, tdec, _ = x_sample.shape
    assert tdec == 1
    h_sb = cache_k_sb.shape[2]
    h_mb = cache_k_moba.shape[2]
    h_fox = cache_k_fox.shape[2]
    npg = page_table.shape[1]
    past = npg * PAGE
    mp, ms = bp * seq, bs * tdec
    tm_p = min(512, mp)
    t_att = MOBA_BLOCK

    hp = x_prompt.reshape(mp, d)
    hs = x_sample.reshape(ms, d)

    w_in0_b, w_out0_b = w_in0.astype(BF16), w_out0.astype(BF16)
    w_up0_b, w_down0_b = w_up0.astype(BF16), w_down0.astype(BF16)
    w_in1_b, w_out1_b = w_in1.astype(BF16), w_out1.astype(BF16)
    w_up1_b, w_down1_b = w_up1.astype(BF16), w_down1.astype(BF16)
    nfox = h_fox * HEAD_DIM
    wf_b = jnp.pad(w_in1[:, 3 * nfox:], ((0, 0), (0, HEAD_DIM - h_fox))).astype(BF16)
    bf_pad = jnp.pad(b_f1, (0, HEAD_DIM - h_fox)).reshape(1, HEAD_DIM)

    tabs_p = _rope_tables(jnp.arange(seq, dtype=jnp.int32))
    tabs_s = _rope_tables(jnp.full((ms,), past, jnp.int32))

    seg0 = h_sb * HEAD_DIM
    qa_p, ka_p, va_p, qb_p, kb_p, vb_p = _project(
        hp, g_mix0, w_in0_b, 6, seg0, (3, 4), tabs_p, tm_p, 256, "proj0_prompt")
    qa_s, ka_s, va_s, qb_s, kb_s, vb_s = _project(
        hs, g_mix0, w_in0_b, 6, seg0, (3, 4), tabs_s, ms, 256, "proj0_sample")

    o_sb_p = _prompt_attention("sb", qa_p, ka_p, va_p, bp, seq, h_sb, t_att, HEAD_GROUP, "sb_prompt")
    o_mb_p = _prompt_attention("moba", qb_p, kb_p, vb_p, bp, seq, h_mb, MOBA_BLOCK, HEAD_GROUP,
                               "moba_prompt")
    hp = _out_project([o_sb_p, o_mb_p], w_out0_b, hp, 1024, 512, "out0_prompt")

    o_sb_s = _sb_sample(qa_s.reshape(bs, h_sb, HEAD_DIM), _pool2d(cache_k_sb), _pool2d(cache_v_sb),
                        page_table, h_sb)
    o_mb_s = _moba_sample(qb_s.reshape(bs, h_mb, HEAD_DIM), kb_s.reshape(bs, h_mb, HEAD_DIM),
                          vb_s.reshape(bs, h_mb, HEAD_DIM), _pool2d(cache_k_moba), _pool2d(cache_v_moba),
                          page_table, h_mb)
    hs = _out_project([o_sb_s.reshape(ms, seg0), o_mb_s.reshape(ms, seg0)], w_out0_b, hs, ms, 512,
                      "out0_sample")

    hp = _mlp(hp, g_mlp0, w_up0_b, w_down0_b, 1024, 512, "mlp0_prompt")
    hs = _mlp(hs, g_mlp0, w_up0_b, w_down0_b, ms, 512, "mlp0_sample")

    q_p, kf_p, vf_p = _project(hp, g_mix1, w_in1_b, 3, nfox, (), None, tm_p, 256, "proj1_prompt")
    q_s, kf_s, vf_s = _project(hs, g_mix1, w_in1_b, 3, nfox, (), None, ms, 256, "proj1_sample")
    lf_p = _forget_gate(hp, g_mix1, wf_b, bf_pad, tm_p, "gate_prompt")
    lf_s = _forget_gate(hs, g_mix1, wf_b, bf_pad, ms, "gate_sample")
    logf_p = lf_p[:, :h_fox]
    logf_s = lf_s[:, :h_fox]

    fcum = _cumsum_rows_t(lf_p, bp, seq, h_fox)
    t_fox = min(FOX_TILE, seq)
    o_fox_p = _prompt_attention("fox", q_p, kf_p, vf_p, bp, seq, h_fox, t_fox, FOX_HEAD_GROUP, "fox_prompt",
                                fcum=fcum.reshape(bp * h_fox, seq // t_fox, 1, t_fox))
    hp = _out_project([o_fox_p], w_out1_b, hp, 1024, 512, "out1_prompt")

    lfn = jnp.broadcast_to(logf_s[:, :, None], (bs, h_fox, HEAD_DIM))
    lf_pool = cache_logf_fox.reshape(cache_logf_fox.shape[0], 1, PAGE * h_fox)
    o_fox_s = _fox_sample(q_s.reshape(bs, h_fox, HEAD_DIM), kf_s.reshape(bs, h_fox, HEAD_DIM),
                          vf_s.reshape(bs, h_fox, HEAD_DIM), lfn, _pool2d(cache_k_fox),
                          _pool2d(cache_v_fox), lf_pool, page_table, h_fox)
    hs = _out_project([o_fox_s.reshape(ms, nfox)], w_out1_b, hs, ms, 512, "out1_sample")

    y_p = _mlp(hp, g_mlp1, w_up1_b, w_down1_b, 1024, 512, "mlp1_prompt", g_final=g_final)
    y_s = _mlp(hs, g_mlp1, w_up1_b, w_down1_b, ms, 512, "mlp1_sample", g_final=g_final)

    r4 = lambda a, b, h: a.reshape(b, -1, h, HEAD_DIM)
    return (y_p.reshape(bp, seq, d), y_s.reshape(bs, tdec, d),
            r4(ka_p, bp, h_sb), r4(ka_s, bs, h_sb), r4(va_p, bp, h_sb), r4(va_s, bs, h_sb),
            r4(kb_p, bp, h_mb), r4(kb_s, bs, h_mb), r4(vb_p, bp, h_mb), r4(vb_s, bs, h_mb),
            r4(kf_p, bp, h_fox), r4(kf_s, bs, h_fox), r4(vf_p, bp, h_fox), r4(vf_s, bs, h_fox),
            logf_p.reshape(bp, seq, h_fox), logf_s.reshape(bs, tdec, h_fox))
```
